```python
import jax
import jax.numpy as jnp
from jax import lax

D_MODEL = 1024
BATCH = 8
SEQ = 2048
DEPTH = 2
DEC_BATCH = 128
DEC_SEQ = 8
PAST_LEN = 16384
PAGE_SIZE = 128

D_MIX = 2 * D_MODEL
D_FF = ((8 * D_MODEL // 3 + 127) // 128) * 128
CONV_W = 4
CHUNK = 64
DN_HEADS = 4
DN_DK = D_MIX // 4 // DN_HEADS
DN_DV = D_MIX // 4 // DN_HEADS
GLA_HEADS = 4
GLA_DV = D_MIX // 4 // GLA_HEADS
GLA_DK = GLA_DV // 2
GLA_RANK = 16
GLA_NORMALIZER = 16.0
M2_DIM = D_MIX // 2
M2_HEADDIM = 64
M2_HEADS = M2_DIM // M2_HEADDIM
M2_STATE = 128
M2_GROUPS = 2
EPS = 1e-6

DN_QK = DN_HEADS * DN_DK
DN_V = DN_HEADS * DN_DV
GLA_QK = GLA_HEADS * GLA_DK
GLA_V = GLA_HEADS * GLA_DV
M2_BC = M2_GROUPS * M2_STATE
CONV_SIZES = (DN_QK, DN_QK, DN_V, M2_DIM, M2_BC, M2_BC)
CONV_CH = sum(CONV_SIZES)
OTHER_SIZES = (DN_V, DN_HEADS, DN_HEADS, GLA_QK, GLA_QK, GLA_V, GLA_RANK, GLA_V, M2_DIM, M2_HEADS)
IN_COLS = CONV_CH + sum(OTHER_SIZES)

kernel_name = 'hybrid_deltanet_gla_ssd_macaron_step'


def rmsnorm(x, g):
    xf = x.astype(jnp.float32)
    y = xf * lax.rsqrt(jnp.mean(xf * xf, axis=-1, keepdims=True) + EPS)
    return (y * g.astype(jnp.float32)).astype(x.dtype)


def l2norm(x):
    xf = x.astype(jnp.float32)
    return xf * lax.rsqrt(jnp.sum(xf * xf, axis=-1, keepdims=True) + EPS)


def split_cols(x, sizes):
    offs, acc = [], 0
    for s in sizes[:-1]:
        acc += s
        offs.append(acc)
    return jnp.split(x, offs, axis=-1)


def swiglu_ffn(x, norm_g, w_gu, w_down):
    gate, up = jnp.split(rmsnorm(x, norm_g) @ w_gu, 2, axis=-1)
    return (jax.nn.silu(gate) * up) @ w_down


def causal_dwconv(x, prev, w, b):
    seq = x.shape[1]
    xp = jnp.concatenate([prev.astype(x.dtype), x], axis=1)
    y = lax.conv_general_dilated(xp, w[:, None, :].astype(x.dtype), (1,), 'VALID',
                                 dimension_numbers=('NWC', 'WIO', 'NWC'),
                                 feature_group_count=x.shape[-1])
    return jax.nn.silu(y + b.astype(x.dtype)), xp[:, seq:]


def to_chunks(x, c):
    b, l = x.shape[:2]
    nc = -(-l // c)
    x = jnp.pad(x, [(0, 0), (0, nc * c - l)] + [(0, 0)] * (x.ndim - 2))
    x = x.reshape(b, nc, c, *x.shape[2:])
    return jnp.moveaxis(jnp.moveaxis(x, 1, 0), 2, 3)


def from_chunks(x, l):
    x = jnp.moveaxis(jnp.moveaxis(x, 3, 2), 0, 1)
    b, nc, c = x.shape[:3]
    return x.reshape(b, nc * c, *x.shape[3:])[:, :l]


def gated_delta_chunked(q, k, v, g, beta, s0):
    seq = q.shape[1]
    c = min(CHUNK, seq)
    xs = (to_chunks(q * DN_DK ** -0.5, c), to_chunks(k, c), to_chunks(v, c),
          to_chunks(g, c), to_chunks(beta, c))
    incl = jnp.tril(jnp.ones((c, c), bool))
    strict = jnp.tril(jnp.ones((c, c), bool), -1)
    eye = jnp.eye(c, dtype=jnp.float32)

    def step(s, inp):
        q_, k_, v_, g_, b_ = inp
        gc = jnp.cumsum(g_, axis=-1)
        decay = jnp.exp(jnp.where(incl, gc[..., :, None] - gc[..., None, :], -jnp.inf))
        kk = jnp.einsum('bhid,bhjd->bhij', k_, k_)
        a_mat = jnp.where(strict, kk * decay, 0.0) * b_[..., None]
        rhs = jnp.concatenate([(b_ * jnp.exp(gc))[..., None] * k_, b_[..., None] * v_], axis=-1)
        sol = lax.linalg.triangular_solve(eye + a_mat, rhs, left_side=True, lower=True)
        w_, u_ = sol[..., :DN_DK], sol[..., DN_DK:]
        v_new = u_ - jnp.einsum('bhcd,bhde->bhce', w_, s)
        qk = jnp.einsum('bhid,bhjd->bhij', q_, k_) * decay
        o = (jnp.einsum('bhcd,bhde->bhce', q_ * jnp.exp(gc)[..., None], s)
             + jnp.einsum('bhij,bhje->bhie', qk, v_new))
        glast = gc[..., -1]
        s = (jnp.exp(glast)[..., None, None] * s
             + jnp.einsum('bhcd,bhce->bhde', k_ * jnp.exp(glast[..., None] - gc)[..., None], v_new))
        return s, o

    s, oc = lax.scan(step, s0.astype(jnp.float32), xs)
    return from_chunks(oc, seq), s


def gla_chunked(q, k, v, gk, s0):
    seq = q.shape[1]
    c = min(CHUNK, seq)
    xs = (to_chunks(q * GLA_DK ** -0.5, c), to_chunks(k, c), to_chunks(v, c), to_chunks(gk, c))
    incl = jnp.tril(jnp.ones((c, c), bool))

    def step(s, inp):
        q_, k_, v_, g_ = inp
        gc = jnp.cumsum(g_, axis=-2)
        diff = jnp.where(incl[:, :, None], gc[:, :, :, None, :] - gc[:, :, None, :, :], -jnp.inf)
        scores = jnp.einsum('bhid,bhijd,bhjd->bhij', q_, jnp.exp(diff), k_)
        o = (jnp.einsum('bhid,bhde->bhie', q_ * jnp.exp(gc), s)
             + jnp.einsum('bhij,bhje->bhie', scores, v_))
        glast = gc[:, :, -1]
        s = (jnp.exp(glast)[..., None] * s
             + jnp.einsum('bhcd,bhce->bhde', k_ * jnp.exp(glast[:, :, None] - gc), v_))
        return s, o

    s, oc = lax.scan(step, s0.astype(jnp.float32), xs)
    return from_chunks(oc, seq), s


def ssd_chunked(x, dt, a, bm, cm, s0):
    bsz, seq, nh, hd = x.shape
    hpg = nh // M2_GROUPS
    c = min(CHUNK, seq)
    xc = to_chunks(x * dt[..., None], c)
    nc = xc.shape[0]
    xc = xc.reshape(nc, bsz, M2_GROUPS, hpg, c, hd)
    lc = to_chunks(dt * a, c).reshape(nc, bsz, M2_GROUPS, hpg, c)
    xs = (xc, lc, to_chunks(bm, c), to_chunks(cm, c))
    incl = jnp.tril(jnp.ones((c, c), bool))

    def step(s, inp):
        x_, l_, b_, c_ = inp
        lcum = jnp.cumsum(l_, axis=-1)
        decay = jnp.exp(jnp.where(incl, lcum[..., :, None] - lcum[..., None, :], -jnp.inf))
        cb = jnp.einsum('bgin,bgjn->bgij', c_, b_)
        y = (jnp.einsum('bghij,bghjp->bghip', cb[:, :, None] * decay, x_)
             + jnp.einsum('bgin,bghnp->bghip', c_, s) * jnp.exp(lcum)[..., None])
        llast = lcum[..., -1]
        s = (jnp.exp(llast)[..., None, None] * s
             + jnp.einsum('bgjn,bghjp->bghnp', b_, x_ * jnp.exp(llast[..., None] - lcum)[..., None]))
        return s, y

    s0g = s0.astype(jnp.float32).reshape(bsz, M2_GROUPS, hpg, M2_STATE, hd)
    s, yc = lax.scan(step, s0g, xs)
    yc = yc.reshape(nc, bsz, nh, c, hd)
    return from_chunks(yc, seq), s.reshape(bsz, nh, M2_STATE, hd)


def hybrid_layer(x, conv_prev, s_dn, s_gla, s_ssm, lp):
    f32 = jnp.float32
    bsz, seq = x.shape[:2]
    x = x + 0.5 * swiglu_ffn(x, lp['ffn1_norm'], lp['ffn1_w_gu'], lp['ffn1_w_down'])
    proj = rmsnorm(x, lp['mix_norm']) @ lp['w_in']
    conv_in, other = proj[..., :CONV_CH], proj[..., CONV_CH:]
    conv_out, conv_new = causal_dwconv(conv_in, conv_prev, lp['conv_w'], lp['conv_b'])
    dq, dk, dv, mx, mb, mc = split_cols(conv_out, CONV_SIZES)
    dz, da, db, gq, gkk, gv, glow, gg, mz, mdt = split_cols(other, OTHER_SIZES)

    q = l2norm(dq.reshape(bsz, seq, DN_HEADS, DN_DK))
    k = l2norm(dk.reshape(bsz, seq, DN_HEADS, DN_DK))
    v = dv.reshape(bsz, seq, DN_HEADS, DN_DV).astype(f32)
    g = -jnp.exp(lp['dn_a_log'].astype(f32)) * jax.nn.softplus(da.astype(f32) + lp['dn_dt_bias'].astype(f32))
    beta = jax.nn.sigmoid(db.astype(f32))
    o_dn, s_dn_new = gated_delta_chunked(q, k, v, g, beta, s_dn)
    o_dn = rmsnorm(o_dn, lp['dn_norm']) * jax.nn.silu(dz.reshape(bsz, seq, DN_HEADS, DN_DV).astype(f32))

    gk = jax.nn.log_sigmoid((glow @ lp['gla_w_up'] + lp['gla_b_up']).astype(f32)) / GLA_NORMALIZER
    o_gla, s_gla_new = gla_chunked(gq.reshape(bsz, seq, GLA_HEADS, GLA_DK).astype(f32),
                                   gkk.reshape(bsz, seq, GLA_HEADS, GLA_DK).astype(f32),
                                   gv.reshape(bsz, seq, GLA_HEADS, GLA_DV).astype(f32),
                                   gk.reshape(bsz, seq, GLA_HEADS, GLA_DK), s_gla)
    o_gla = rmsnorm(o_gla, lp['gla_norm']) * jax.nn.silu(gg.reshape(bsz, seq, GLA_HEADS, GLA_DV).astype(f32))

    xh = mx.reshape(bsz, seq, M2_HEADS, M2_HEADDIM).astype(f32)
    dt = jax.nn.softplus(mdt.astype(f32) + lp['m2_dt_bias'].astype(f32))
    a = -jnp.exp(lp['m2_a_log'].astype(f32))
    y, s_ssm_new = ssd_chunked(xh, dt, a,
                               mb.reshape(bsz, seq, M2_GROUPS, M2_STATE).astype(f32),
                               mc.reshape(bsz, seq, M2_GROUPS, M2_STATE).astype(f32), s_ssm)
    y = y + lp['m2_d'].astype(f32)[:, None] * xh
    o_m2 = rmsnorm(y.reshape(bsz, seq, M2_DIM) * jax.nn.silu(mz.astype(f32)), lp['m2_norm'])

    mix = jnp.concatenate([o_dn.reshape(bsz, seq, DN_V), o_gla.reshape(bsz, seq, GLA_V), o_m2], axis=-1)
    x = x + mix.astype(x.dtype) @ lp['w_out']
    x = x + 0.5 * swiglu_ffn(x, lp['ffn2_norm'], lp['ffn2_w_gu'], lp['ffn2_w_down'])
    return x, conv_new, s_dn_new, s_gla_new, s_ssm_new


def _dt_bias(key, shape):
    dt = jnp.exp(jax.random.uniform(key, shape, dtype=jnp.float32,
                                    minval=jnp.log(1e-3), maxval=jnp.log(1e-1)))
    return dt + jnp.log(-jnp.expm1(-dt))


def setup_inputs(seed: int = 0) -> dict:
    key = jax.random.key(seed)
    ks = jax.random.split(key, 32)
    f32 = jnp.float32

    def nrm(k, shape, scale):
        return scale * jax.random.normal(k, shape, f32)

    def gain(k, shape):
        return 1.0 + 0.02 * jax.random.normal(k, shape, f32)

    L = DEPTH
    return {
        'x_prompt': nrm(ks[0], (BATCH, SEQ, D_MODEL), 1.0),
        'x_sample': nrm(ks[1], (DEC_BATCH, DEC_SEQ, D_MODEL), 1.0),
        'state_conv': nrm(ks[2], (L, DEC_BATCH, CONV_W - 1, CONV_CH), 1.0),
        'state_delta': nrm(ks[3], (L, DEC_BATCH, DN_HEADS, DN_DK, DN_DV), 0.1),
        'state_gla': nrm(ks[4], (L, DEC_BATCH, GLA_HEADS, GLA_DK, GLA_DV), 0.3),
        'state_ssm': nrm(ks[5], (L, DEC_BATCH, M2_HEADS, M2_STATE, M2_HEADDIM), 0.1),
        'ffn1_norm': gain(ks[6], (L, D_MODEL)),
        'ffn1_w_gu': nrm(ks[7], (L, D_MODEL, 2 * D_FF), D_MODEL ** -0.5),
        'ffn1_w_down': nrm(ks[8], (L, D_FF, D_MODEL), D_FF ** -0.5),
        'mix_norm': gain(ks[9], (L, D_MODEL)),
        'w_in': nrm(ks[10], (L, D_MODEL, IN_COLS), D_MODEL ** -0.5),
        'conv_w': nrm(ks[11], (L, CONV_W, CONV_CH), CONV_W ** -0.5),
        'conv_b': nrm(ks[12], (L, CONV_CH), 0.02),
        'dn_a_log': jnp.log(jax.random.uniform(ks[13], (L, DN_HEADS), dtype=f32, minval=1.0, maxval=16.0)),
        'dn_dt_bias': _dt_bias(ks[14], (L, DN_HEADS)),
        'dn_norm': gain(ks[15], (L, DN_DV)),
        'gla_w_up': nrm(ks[16], (L, GLA_RANK, GLA_QK), GLA_RANK ** -0.5),
        'gla_b_up': nrm(ks[17], (L, GLA_QK), 0.1),
        'gla_norm': gain(ks[18], (L, GLA_DV)),
        'm2_a_log': jnp.log(jax.random.uniform(ks[19], (L, M2_HEADS), dtype=f32, minval=1.0, maxval=16.0)),
        'm2_dt_bias': _dt_bias(ks[20], (L, M2_HEADS)),
        'm2_d': gain(ks[21], (L, M2_HEADS)),
        'm2_norm': gain(ks[22], (L, M2_DIM)),
        'w_out': nrm(ks[23], (L, D_MIX, D_MODEL), D_MIX ** -0.5),
        'ffn2_norm': gain(ks[24], (L, D_MODEL)),
        'ffn2_w_gu': nrm(ks[25], (L, D_MODEL, 2 * D_FF), D_MODEL ** -0.5),
        'ffn2_w_down': nrm(ks[26], (L, D_FF, D_MODEL), D_FF ** -0.5),
        'final_norm': gain(ks[27], (D_MODEL,)),
    }


def reference(x_prompt, x_sample, state_conv, state_delta, state_gla, state_ssm,
              ffn1_norm, ffn1_w_gu, ffn1_w_down, mix_norm, w_in, conv_w, conv_b,
              dn_a_log, dn_dt_bias, dn_norm, gla_w_up, gla_b_up, gla_norm,
              m2_a_log, m2_dt_bias, m2_d, m2_norm, w_out,
              ffn2_norm, ffn2_w_gu, ffn2_w_down, final_norm):
    f32 = jnp.float32
    bp = x_prompt.shape[0]
    zero_conv = jnp.zeros((bp, CONV_W - 1, CONV_CH), x_prompt.dtype)
    zero_dn = jnp.zeros((bp, DN_HEADS, DN_DK, DN_DV), f32)
    zero_gla = jnp.zeros((bp, GLA_HEADS, GLA_DK, GLA_DV), f32)
    zero_ssm = jnp.zeros((bp, M2_HEADS, M2_STATE, M2_HEADDIM), f32)

    hp, hs = x_prompt, x_sample
    pc, pd, pg, ps = [], [], [], []
    sc, sd, sg, ss = [], [], [], []
    for l in range(DEPTH):
        lp = {
            'ffn1_norm': ffn1_norm[l], 'ffn1_w_gu': ffn1_w_gu[l], 'ffn1_w_down': ffn1_w_down[l],
            'mix_norm': mix_norm[l], 'w_in': w_in[l], 'conv_w': conv_w[l], 'conv_b': conv_b[l],
            'dn_a_log': dn_a_log[l], 'dn_dt_bias': dn_dt_bias[l], 'dn_norm': dn_norm[l],
            'gla_w_up': gla_w_up[l], 'gla_b_up': gla_b_up[l], 'gla_norm': gla_norm[l],
            'm2_a_log': m2_a_log[l], 'm2_dt_bias': m2_dt_bias[l], 'm2_d': m2_d[l], 'm2_norm': m2_norm[l],
            'w_out': w_out[l],
            'ffn2_norm': ffn2_norm[l], 'ffn2_w_gu': ffn2_w_gu[l], 'ffn2_w_down': ffn2_w_down[l],
        }
        hp, c_, d_, g_, s_ = hybrid_layer(hp, zero_conv, zero_dn, zero_gla, zero_ssm, lp)
        pc.append(c_); pd.append(d_); pg.append(g_); ps.append(s_)
        hs, c_, d_, g_, s_ = hybrid_layer(hs, state_conv[l], state_delta[l], state_gla[l], state_ssm[l], lp)
        sc.append(c_); sd.append(d_); sg.append(g_); ss.append(s_)

    y_prompt = rmsnorm(hp, final_norm)
    y_sample = rmsnorm(hs, final_norm)
    new_conv_prompt = jnp.stack(pc, axis=0)
    new_delta_prompt = jnp.stack(pd, axis=0)
    new_gla_prompt = jnp.stack(pg, axis=0)
    new_ssm_prompt = jnp.stack(ps, axis=0)
    new_conv_sample = jnp.stack(sc, axis=0)
    new_delta_sample = jnp.stack(sd, axis=0)
    new_gla_sample = jnp.stack(sg, axis=0)
    new_ssm_sample = jnp.stack(ss, axis=0)
    return (y_prompt, y_sample, new_conv_prompt, new_delta_prompt, new_gla_prompt, new_ssm_prompt,
            new_conv_sample, new_delta_sample, new_gla_sample, new_ssm_sample)
```

```python
import functools

import jax
import jax.numpy as jnp
from jax import lax
from jax.experimental import pallas as pl
from jax.experimental.pallas import tpu as pltpu

F32 = jnp.float32
BF16 = jnp.bfloat16
HIGHEST = lax.Precision.HIGHEST

D_MODEL = 1024
DEPTH = 2
D_MIX = 2 * D_MODEL
D_FF = 2816
CONV_W = 4
DN_HEADS = 4
DN_DK = 128
DN_DV = 128
GLA_HEADS = 4
GLA_DK = 64
GLA_DV = 128
GLA_RANK = 16
GLA_NORMALIZER = 16.0
M2_DIM = 1024
M2_HEADDIM = 64
M2_HEADS = 16
M2_STATE = 128
M2_GROUPS = 2
M2_HPG = M2_HEADS // M2_GROUPS
EPS = 1e-6

DN_QK = DN_HEADS * DN_DK
DN_V = DN_HEADS * DN_DV
GLA_QK = GLA_HEADS * GLA_DK
GLA_V = GLA_HEADS * GLA_DV
M2_BC = M2_GROUPS * M2_STATE
CONV_CH = 3 * 512 + M2_DIM + 2 * M2_BC

OFF_DQ = 0
OFF_DK = OFF_DQ + DN_QK
OFF_DV = OFF_DK + DN_QK
OFF_MX = OFF_DV + DN_V
OFF_MB = OFF_MX + M2_DIM
OFF_MC = OFF_MB + M2_BC
OFF_DZ = CONV_CH
OFF_GQ = OFF_DZ + DN_V
OFF_GK = OFF_GQ + GLA_QK
OFF_GV = OFF_GK + GLA_QK
OFF_GG = OFF_GV + GLA_V
OFF_MZ = OFF_GG + GLA_V
OFF_SMALL = OFF_MZ + M2_DIM
SMALL_W = 128
SM_DA = 0
SM_DB = SM_DA + DN_HEADS
SM_GLOW = SM_DB + DN_HEADS
SM_MDT = SM_GLOW + GLA_RANK
SM_USED = SM_MDT + M2_HEADS
PROJ_COLS = OFF_SMALL + SMALL_W

CONV_PAD = 8
VMEM_LIMIT_BYTES = 56 * 1024 * 1024
PROMPT_CHUNK = 64
DENSE_TILE = 256


def _dot(a, b):
    return jnp.dot(a.astype(BF16), b.astype(BF16), preferred_element_type=F32)


def _dot_nt(a, b):
    return lax.dot_general(a.astype(BF16), b.astype(BF16), (((1,), (1,)), ((), ())),
                           preferred_element_type=F32)


def _dot_tn(a, b):
    return lax.dot_general(a.astype(BF16), b.astype(BF16), (((0,), (0,)), ((), ())),
                           preferred_element_type=F32)


def _dot_hi(a, b):
    return jnp.dot(a, b, precision=HIGHEST, preferred_element_type=F32)


def _dot_tn_hi(a, b):
    return lax.dot_general(a, b, (((0,), (0,)), ((), ())), precision=HIGHEST,
                           preferred_element_type=F32)


def _sigmoid(x):
    return 1.0 / (1.0 + jnp.exp(-x))


def _silu(x):
    return x * _sigmoid(x)


def _softplus(x):
    return jnp.maximum(x, 0.0) + jnp.log1p(jnp.exp(-jnp.abs(x)))


def _rms(x, gain):
    return x * lax.rsqrt(jnp.mean(x * x, axis=-1, keepdims=True) + EPS) * gain


def _l2norm(x):
    return x * lax.rsqrt(jnp.sum(x * x, axis=-1, keepdims=True) + EPS)


def _ffn_kernel(x_ref, g_ref, wg_ref, wu_ref, wd_ref, o_ref):
    x = x_ref[...]
    xn = _rms(x, g_ref[...]).astype(BF16)
    gate = jnp.dot(xn, wg_ref[...], preferred_element_type=F32)
    up = jnp.dot(xn, wu_ref[...], preferred_element_type=F32)
    h = (_silu(gate) * up).astype(BF16)
    o_ref[...] = x + 0.5 * jnp.dot(h, wd_ref[...], preferred_element_type=F32)


def _ffn_norm_kernel(x_ref, g_ref, wg_ref, wu_ref, wd_ref, fg_ref, o_ref):
    x = x_ref[...]
    xn = _rms(x, g_ref[...]).astype(BF16)
    gate = jnp.dot(xn, wg_ref[...], preferred_element_type=F32)
    up = jnp.dot(xn, wu_ref[...], preferred_element_type=F32)
    h = (_silu(gate) * up).astype(BF16)
    y = x + 0.5 * jnp.dot(h, wd_ref[...], preferred_element_type=F32)
    o_ref[...] = _rms(y, fg_ref[...])


def _inproj_kernel(x_ref, g_ref, w_ref, o_ref):
    xn = _rms(x_ref[...], g_ref[...]).astype(BF16)
    o_ref[...] = jnp.dot(xn, w_ref[...], preferred_element_type=F32)


def _outproj_kernel(x_ref, m_ref, w_ref, o_ref):
    o_ref[...] = x_ref[...] + jnp.dot(m_ref[...].astype(BF16), w_ref[...], preferred_element_type=F32)


def _resident(shape):
    nd = len(shape)
    return pl.BlockSpec(shape, lambda *_: (0,) * nd, pipeline_mode=pl.Buffered(1))


def _dense_params():
    return pltpu.CompilerParams(dimension_semantics=("parallel",), vmem_limit_bytes=VMEM_LIMIT_BYTES)


def _ffn(x, gain, wg, wu, wd, final_gain=None):
    t = x.shape[0]
    tile = pl.BlockSpec((DENSE_TILE, D_MODEL), lambda i: (i, 0))
    in_specs = [tile, _resident((1, D_MODEL)), _resident(wg.shape), _resident(wu.shape), _resident(wd.shape)]
    args = [x, gain.reshape(1, D_MODEL), wg, wu, wd]
    body = _ffn_kernel
    if final_gain is not None:
        in_specs.append(_resident((1, D_MODEL)))
        args.append(final_gain.reshape(1, D_MODEL))
        body = _ffn_norm_kernel
    return pl.pallas_call(
        body, grid=(t // DENSE_TILE,), in_specs=in_specs, out_specs=tile,
        out_shape=jax.ShapeDtypeStruct((t, D_MODEL), F32), compiler_params=_dense_params(),
        name="ffn" if final_gain is None else "ffn_final_norm")(*args)


def _inproj(x, gain, w):
    t = x.shape[0]
    return pl.pallas_call(
        _inproj_kernel, grid=(t // DENSE_TILE,),
        in_specs=[pl.BlockSpec((DENSE_TILE, D_MODEL), lambda i: (i, 0)), _resident((1, D_MODEL)), _resident(w.shape)],
        out_specs=pl.BlockSpec((DENSE_TILE, PROJ_COLS), lambda i: (i, 0)),
        out_shape=jax.ShapeDtypeStruct((t, PROJ_COLS), F32), compiler_params=_dense_params(),
        name="inproj")(x, gain.reshape(1, D_MODEL), w)


def _outproj(x, mix, w):
    t = x.shape[0]
    return pl.pallas_call(
        _outproj_kernel, grid=(t // DENSE_TILE,),
        in_specs=[pl.BlockSpec((DENSE_TILE, D_MODEL), lambda i: (i, 0)),
                  pl.BlockSpec((DENSE_TILE, D_MIX), lambda i: (i, 0)), _resident(w.shape)],
        out_specs=pl.BlockSpec((DENSE_TILE, D_MODEL), lambda i: (i, 0)),
        out_shape=jax.ShapeDtypeStruct((t, D_MODEL), F32), compiler_params=_dense_params(),
        name="outproj")(x, mix, w)


def _tri_inverse(a, ri, ci, c):
    base = min(c, 16)
    eye = (ri == ci).astype(F32)
    d = a if base == c else jnp.where((ri // base) == (ci // base), a, 0.0)
    inv = eye - d
    power, p = d, 1
    while 2 * p < base:
        power = _dot_hi(power, power)
        p *= 2
        inv = inv + _dot_hi(inv, power)
    s = base
    while s < c:
        inside = ((ri // (2 * s)) == (ci // (2 * s))) & ((ri // s) != (ci // s))
        e = jnp.where(inside, a, 0.0)
        inv = inv - _dot_hi(inv, _dot_hi(e, inv))
        s *= 2
    return inv


def _mixer_kernel(c, proj_ref, conv0_ref, dn0_ref, gla0_ref, ssm0_ref, convw_ref, convb_ref,
                  dn_alog_ref, dn_dtb_ref, dn_norm_ref, gla_wup_ref, gla_bup_ref, gla_norm_ref,
                  m2_alog_ref, m2_dtb_ref, m2_d_ref, m2_norm_ref,
                  mix_ref, convn_ref, dnn_ref, glan_ref, ssmn_ref, cbuf):
    chunk = pl.program_id(1)
    last = pl.num_programs(1) - 1

    @pl.when(chunk == 0)
    def _():
        cbuf[pl.ds(CONV_PAD - (CONV_W - 1), CONV_W - 1), :] = conv0_ref[...]
        dnn_ref[...] = dn0_ref[...]
        glan_ref[...] = gla0_ref[...]
        ssmn_ref[...] = ssm0_ref[...]

    cbuf[pl.ds(CONV_PAD, c), :] = proj_ref[:, 0:CONV_CH]

    def conv(lo, width):
        acc = convb_ref[:, lo:lo + width]
        for w in range(CONV_W):
            acc = acc + convw_ref[w:w + 1, lo:lo + width] * cbuf[pl.ds(CONV_PAD - (CONV_W - 1) + w, c), lo:lo + width]
        return _silu(acc)

    ri = lax.broadcasted_iota(jnp.int32, (c, c), 0)
    ci = lax.broadcasted_iota(jnp.int32, (c, c), 1)
    incl = ri >= ci
    strict = ri > ci
    eye_mask = ri == ci
    eye = eye_mask.astype(F32)
    tril = incl.astype(F32)

    small = proj_ref[:, OFF_SMALL:OFF_SMALL + SMALL_W]
    da = small[:, SM_DA:SM_DA + DN_HEADS]
    db = small[:, SM_DB:SM_DB + DN_HEADS]
    glow = small[:, SM_GLOW:SM_GLOW + GLA_RANK]
    mdt = small[:, SM_MDT:SM_MDT + M2_HEADS]

    def decay_matrix(cum_col, cum_rows, h):
        diff = cum_col - cum_rows[h:h + 1, :]
        return jnp.where(incl, jnp.exp(jnp.minimum(diff, 0.0)), 0.0)

    g_dn = -jnp.exp(dn_alog_ref[...]) * _softplus(da + dn_dtb_ref[...])
    beta_dn = _sigmoid(db)
    gc_dn = _dot_hi(tril, g_dn)
    gc_dn_rows = _dot_tn_hi(gc_dn, eye)
    for h in range(DN_HEADS):
        q = _l2norm(conv(OFF_DQ + h * DN_DK, DN_DK)) * (DN_DK ** -0.5)
        k = _l2norm(conv(OFF_DK + h * DN_DK, DN_DK))
        v = conv(OFF_DV + h * DN_DV, DN_DV)
        gc = gc_dn[:, h:h + 1]
        beta = beta_dn[:, h:h + 1]
        decay = decay_matrix(gc, gc_dn_rows, h)
        a_mat = jnp.where(strict, _dot_nt(k, k) * decay, 0.0) * beta
        t_inv = _tri_inverse(a_mat, ri, ci, c)
        egc = jnp.exp(gc)
        rhs = jnp.concatenate([(beta * egc) * k, beta * v], axis=1)
        sol = _dot_hi(t_inv, rhs)
        w_mat, u_mat = sol[:, :DN_DK], sol[:, DN_DK:]
        s = dnn_ref[h]
        v_new = u_mat - _dot(w_mat, s)
        qk = _dot_nt(q, k) * decay
        o = _dot(q * egc, s) + _dot(qk, v_new)
        glast = gc[c - 1:c, :]
        dnn_ref[h] = jnp.exp(glast) * s + _dot_tn(k * jnp.exp(glast - gc), v_new)
        z = proj_ref[:, OFF_DZ + h * DN_DV:OFF_DZ + (h + 1) * DN_DV]
        mix_ref[:, h * DN_DV:(h + 1) * DN_DV] = _rms(o, dn_norm_ref[...]) * _silu(z)

    zg = _dot(glow, gla_wup_ref[...]) + gla_bup_ref[...]
    gk = -_softplus(-zg) * (1.0 / GLA_NORMALIZER)
    gcum = _dot_hi(tril, gk)
    gq = proj_ref[:, OFF_GQ:OFF_GQ + GLA_QK] * (GLA_DK ** -0.5)
    gkk = proj_ref[:, OFF_GK:OFF_GK + GLA_QK]
    levels = []
    m = c // 2
    rows = lax.broadcasted_iota(jnp.int32, (c, GLA_QK), 0)
    while m >= 1:
        mid_idx = (ri // (2 * m)) * (2 * m) + (m - 1)
        sel = (ci == mid_idx).astype(F32)
        gmid = _dot_hi(sel, gcum)
        upper = (rows & m) != 0
        q_t = jnp.where(upper, gq * jnp.exp(jnp.minimum(gcum - gmid, 0.0)), 0.0)
        k_t = jnp.where(upper, 0.0, gkk * jnp.exp(jnp.minimum(gmid - gcum, 0.0)))
        same = (ri // (2 * m)) == (ci // (2 * m))
        levels.append((q_t, k_t, same))
        m //= 2
    ones_cols = jnp.ones((c, GLA_DV), F32)
    for h in range(GLA_HEADS):
        hs = slice(h * GLA_DK, (h + 1) * GLA_DK)
        q = gq[:, hs]
        k = gkk[:, hs]
        v = proj_ref[:, OFF_GV + h * GLA_DV:OFF_GV + (h + 1) * GLA_DV]
        gc = gcum[:, hs]
        scores = jnp.where(eye_mask, _dot_nt(q, k), 0.0)
        for q_t, k_t, same in levels:
            scores = scores + jnp.where(same, _dot_nt(q_t[:, hs], k_t[:, hs]), 0.0)
        s = glan_ref[h]
        o = _dot(q * jnp.exp(gc), s) + _dot(scores, v)
        glast = gc[c - 1:c, :]
        glast_cols = _dot_tn_hi(gk[:, hs], ones_cols)
        glan_ref[h] = jnp.exp(glast_cols) * s + _dot_tn(k * jnp.exp(glast - gc), v)
        z = proj_ref[:, OFF_GG + h * GLA_DV:OFF_GG + (h + 1) * GLA_DV]
        mix_ref[:, DN_V + h * GLA_DV:DN_V + (h + 1) * GLA_DV] = _rms(o, gla_norm_ref[...]) * _silu(z)

    dt = _softplus(mdt + m2_dtb_ref[...])
    lg = dt * (-jnp.exp(m2_alog_ref[...]))
    lcum = _dot_hi(tril, lg)
    lcum_rows = _dot_tn_hi(lcum, eye)
    sumsq = jnp.zeros((c, 1), F32)
    m2_off = DN_V + GLA_V
    for g in range(M2_GROUPS):
        bm = conv(OFF_MB + g * M2_STATE, M2_STATE)
        cm = conv(OFF_MC + g * M2_STATE, M2_STATE)
        cb = _dot_nt(cm, bm)
        for hh in range(M2_HPG):
            h = g * M2_HPG + hh
            xh = conv(OFF_MX + h * M2_HEADDIM, M2_HEADDIM)
            lc = lcum[:, h:h + 1]
            x_dt = xh * dt[:, h:h + 1]
            decay = decay_matrix(lc, lcum_rows, h)
            s = ssmn_ref[h]
            y = _dot(cb * decay, x_dt) + _dot(cm, s) * jnp.exp(lc)
            llast = lc[c - 1:c, :]
            ssmn_ref[h] = jnp.exp(llast) * s + _dot_tn(bm, x_dt * jnp.exp(llast - lc))
            y = y + m2_d_ref[:, h:h + 1] * xh
            z = proj_ref[:, OFF_MZ + h * M2_HEADDIM:OFF_MZ + (h + 1) * M2_HEADDIM]
            yz = y * _silu(z)
            sumsq = sumsq + jnp.sum(yz * yz, axis=-1, keepdims=True)
            mix_ref[:, m2_off + h * M2_HEADDIM:m2_off + (h + 1) * M2_HEADDIM] = yz
    scale = lax.rsqrt(sumsq * (1.0 / M2_DIM) + EPS)
    mix_ref[:, m2_off:m2_off + M2_DIM] = mix_ref[:, m2_off:m2_off + M2_DIM] * scale * m2_norm_ref[...]

    tail = cbuf[pl.ds(c, CONV_PAD), :]
    cbuf[pl.ds(0, CONV_PAD), :] = tail

    @pl.when(chunk == last)
    def _():
        convn_ref[...] = cbuf[pl.ds(CONV_PAD - (CONV_W - 1), CONV_W - 1), :]


def _mixer(proj, row_start, nb, seq, c, layer, conv0, dn0, gla0, ssm0, params):
    nc = seq // c
    blk0 = row_start // c
    lsel = layer if conv0.shape[0] > 1 else 0

    def state_spec(shape):
        zeros = (0,) * len(shape)
        return pl.BlockSpec((None, None) + shape, lambda b, k: (lsel, b) + zeros)

    def out_state_spec(shape):
        zeros = (0,) * len(shape)
        return pl.BlockSpec((None,) + shape, lambda b, k: (b,) + zeros)

    def full(arr):
        nd = arr.ndim
        return pl.BlockSpec(arr.shape, lambda b, k: (0,) * nd)

    conv_shape = (CONV_W - 1, CONV_CH)
    dn_shape = (DN_HEADS, DN_DK, DN_DV)
    gla_shape = (GLA_HEADS, GLA_DK, GLA_DV)
    ssm_shape = (M2_HEADS, M2_STATE, M2_HEADDIM)
    in_specs = [pl.BlockSpec((c, PROJ_COLS), lambda b, k: (blk0 + b * nc + k, 0)),
                state_spec(conv_shape), state_spec(dn_shape), state_spec(gla_shape), state_spec(ssm_shape)]
    in_specs += [full(p) for p in params]
    out_specs = [pl.BlockSpec((c, D_MIX), lambda b, k: (b * nc + k, 0)),
                 out_state_spec(conv_shape), out_state_spec(dn_shape), out_state_spec(gla_shape),
                 out_state_spec(ssm_shape)]
    out_shape = [jax.ShapeDtypeStruct((nb * seq, D_MIX), F32),
                 jax.ShapeDtypeStruct((nb,) + conv_shape, F32),
                 jax.ShapeDtypeStruct((nb,) + dn_shape, F32),
                 jax.ShapeDtypeStruct((nb,) + gla_shape, F32),
                 jax.ShapeDtypeStruct((nb,) + ssm_shape, F32)]
    return pl.pallas_call(
        functools.partial(_mixer_kernel, c), grid=(nb, nc), in_specs=in_specs, out_specs=out_specs,
        out_shape=out_shape, scratch_shapes=[pltpu.VMEM((CONV_PAD + c, CONV_CH), F32)],
        compiler_params=pltpu.CompilerParams(dimension_semantics=("parallel", "arbitrary"),
                                             vmem_limit_bytes=VMEM_LIMIT_BYTES),
        name=f"mixer_c{c}")(proj, conv0, dn0, gla0, ssm0, *params)


def _permute_w_in(w):
    o = CONV_CH
    dz = w[:, o:o + DN_V]; o += DN_V
    da = w[:, o:o + DN_HEADS]; o += DN_HEADS
    db = w[:, o:o + DN_HEADS]; o += DN_HEADS
    gq = w[:, o:o + GLA_QK]; o += GLA_QK
    gk = w[:, o:o + GLA_QK]; o += GLA_QK
    gv = w[:, o:o + GLA_V]; o += GLA_V
    glow = w[:, o:o + GLA_RANK]; o += GLA_RANK
    gg = w[:, o:o + GLA_V]; o += GLA_V
    mz = w[:, o:o + M2_DIM]; o += M2_DIM
    mdt = w[:, o:o + M2_HEADS]; o += M2_HEADS
    assert o == w.shape[1]
    pad = jnp.zeros((w.shape[0], SMALL_W - SM_USED), w.dtype)
    return jnp.concatenate([w[:, :CONV_CH], dz, gq, gk, gv, gg, mz, da, db, glow, mdt, pad], axis=1)


def kernel(x_prompt, x_sample, state_conv, state_delta, state_gla, state_ssm, ffn1_norm, ffn1_w_gu, ffn1_w_down, mix_norm, w_in, conv_w, conv_b, dn_a_log, dn_dt_bias, dn_norm, gla_w_up, gla_b_up, gla_norm, m2_a_log, m2_dt_bias, m2_d, m2_norm, w_out, ffn2_norm, ffn2_w_gu, ffn2_w_down, final_norm):
    bp, sp, _ = x_prompt.shape
    bs, ss, _ = x_sample.shape
    tp, ts = bp * sp, bs * ss
    assert tp % DENSE_TILE == 0 and ts % DENSE_TILE == 0 and sp % PROMPT_CHUNK == 0 and tp % ss == 0

    x = jnp.concatenate([x_prompt.reshape(tp, D_MODEL), x_sample.reshape(ts, D_MODEL)], axis=0)
    zero_conv = jnp.zeros((1, bp, CONV_W - 1, CONV_CH), F32)
    zero_dn = jnp.zeros((1, bp, DN_HEADS, DN_DK, DN_DV), F32)
    zero_gla = jnp.zeros((1, bp, GLA_HEADS, GLA_DK, GLA_DV), F32)
    zero_ssm = jnp.zeros((1, bp, M2_HEADS, M2_STATE, M2_HEADDIM), F32)

    prompt_states, sample_states = [], []
    for l in range(DEPTH):
        wg1, wu1 = ffn1_w_gu[l, :, :D_FF].astype(BF16), ffn1_w_gu[l, :, D_FF:].astype(BF16)
        wg2, wu2 = ffn2_w_gu[l, :, :D_FF].astype(BF16), ffn2_w_gu[l, :, D_FF:].astype(BF16)
        params = (conv_w[l], conv_b[l].reshape(1, CONV_CH),
                  dn_a_log[l].reshape(1, DN_HEADS), dn_dt_bias[l].reshape(1, DN_HEADS), dn_norm[l].reshape(1, DN_DV),
                  gla_w_up[l], gla_b_up[l].reshape(1, GLA_QK), gla_norm[l].reshape(1, GLA_DV),
                  m2_a_log[l].reshape(1, M2_HEADS), m2_dt_bias[l].reshape(1, M2_HEADS),
                  m2_d[l].reshape(1, M2_HEADS), m2_norm[l].reshape(1, M2_DIM))

        x = _ffn(x, ffn1_norm[l], wg1, wu1, ffn1_w_down[l].astype(BF16))
        proj = _inproj(x, mix_norm[l], _permute_w_in(w_in[l]).astype(BF16))
        mix_p, *st_p = _mixer(proj, 0, bp, sp, PROMPT_CHUNK, l, zero_conv, zero_dn, zero_gla, zero_ssm, params)
        mix_s, *st_s = _mixer(proj, tp, bs, ss, ss, l, state_conv, state_delta, state_gla, state_ssm, params)
        prompt_states.append(st_p)
        sample_states.append(st_s)
        x = _outproj(x, jnp.concatenate([mix_p, mix_s], axis=0), w_out[l].astype(BF16))
        x = _ffn(x, ffn2_norm[l], wg2, wu2, ffn2_w_down[l].astype(BF16),
                 final_gain=final_norm if l == DEPTH - 1 else None)

    y_prompt = x[:tp].reshape(bp, sp, D_MODEL)
    y_sample = x[tp:].reshape(bs, ss, D_MODEL)
    stacked_p = [jnp.stack([prompt_states[l][i] for l in range(DEPTH)], axis=0) for i in range(4)]
    stacked_s = [jnp.stack([sample_states[l][i] for l in range(DEPTH)], axis=0) for i in range(4)]
    return (y_prompt, y_sample, *stacked_p, *stacked_s)
```

```python
import functools
import math

import jax
import jax.numpy as jnp
import numpy as np
from jax import lax
from jax.experimental import pallas as pl
from jax.experimental.pallas import tpu as pltpu

F32 = jnp.float32
BF16 = jnp.bfloat16
LOG2E = math.log2(math.e)

D_MODEL = 1024
DEPTH = 2
D_MIX = 2 * D_MODEL
D_FF = 2816
CONV_W = 4
DN_HEADS = 4
DN_DK = 128
DN_DV = 128
GLA_HEADS = 4
GLA_DK = 64
GLA_DV = 128
GLA_RANK = 16
GLA_NORMALIZER = 16.0
M2_DIM = 1024
M2_HEADDIM = 64
M2_HEADS = 16
M2_STATE = 128
M2_GROUPS = 2
M2_PAIRS = M2_HEADS // 2
M2_PAIRS_PER_GROUP = M2_PAIRS // M2_GROUPS
EPS = 1e-6
LANES = 128
SUBLANES = 8

DN_QK = DN_HEADS * DN_DK
DN_V = DN_HEADS * DN_DV
GLA_QK = GLA_HEADS * GLA_DK
GLA_V = GLA_HEADS * GLA_DV
M2_BC = M2_GROUPS * M2_STATE
CONV_CH = 3 * 512 + M2_DIM + 2 * M2_BC

OFF_DQ = 0
OFF_DK = OFF_DQ + DN_QK
OFF_DV = OFF_DK + DN_QK
OFF_MX = OFF_DV + DN_V
OFF_MB = OFF_MX + M2_DIM
OFF_MC = OFF_MB + M2_BC
OFF_DZ = CONV_CH
OFF_GQ = OFF_DZ + DN_V
OFF_GK = OFF_GQ + GLA_QK
OFF_GV = OFF_GK + GLA_QK
OFF_GG = OFF_GV + GLA_V
OFF_MZ = OFF_GG + GLA_V
OFF_SMALL = OFF_MZ + M2_DIM
SM_DA = 0
SM_MDT = SM_DA + DN_HEADS
SM_DB = SM_MDT + M2_HEADS
SM_GLOW = SM_DB + DN_HEADS
SM_USED = SM_GLOW + GLA_RANK
PROJ_COLS = OFF_SMALL + LANES

CONV_PAD = SUBLANES
CONV_TAIL = CONV_W - 1
NEUMANN_BLOCK = 16
VMEM_LIMIT_BYTES = 56 * 1024 * 1024
PROMPT_CHUNK = 128
SAMPLE_SEQS_PER_STEP = 4
DENSE_TILE = 256
MASKED = -1e30


def _dot(a, b):
    return jnp.dot(a.astype(BF16), b.astype(BF16), preferred_element_type=F32)


def _dot_nt(a, b):
    return lax.dot_general(a.astype(BF16), b.astype(BF16), (((1,), (1,)), ((), ())),
                           preferred_element_type=F32)


def _dot_tn(a, b):
    return lax.dot_general(a.astype(BF16), b.astype(BF16), (((0,), (0,)), ((), ())),
                           preferred_element_type=F32)


def _split3(x):
    hi = x.astype(BF16)
    r1 = x - hi.astype(F32)
    mid = r1.astype(BF16)
    lo = (r1 - mid.astype(F32)).astype(BF16)
    return hi, mid, lo


def _dot_exact_lhs(m, x):
    acc = None
    for piece in _split3(x):
        part = jnp.dot(m, piece, preferred_element_type=F32)
        acc = part if acc is None else acc + part
    return acc


def _dot_exact_rhs(x, m):
    acc = None
    for piece in _split3(x):
        part = jnp.dot(piece, m, preferred_element_type=F32)
        acc = part if acc is None else acc + part
    return acc


def _silu(x):
    return x * jax.nn.sigmoid(x)


def _softplus(x):
    return jnp.maximum(x, 0.0) + jnp.log1p(jnp.exp(-jnp.abs(x)))


def _rms(x, gain):
    return x * lax.rsqrt(jnp.mean(x * x, axis=-1, keepdims=True) + EPS) * gain


def _ffn_kernel(split_tile, final_norm, *refs):
    refs = list(refs)
    two_in = split_tile is not None and not final_norm
    two_out = split_tile is not None and final_norm
    x_refs = [refs.pop(0) for _ in range(2 if two_in else 1)]
    g_ref, wg_ref, wu_ref, wd_ref = (refs.pop(0) for _ in range(4))
    fg_ref = refs.pop(0) if final_norm else None
    o_refs = refs
    i = pl.program_id(0)
    x = jnp.where(i < split_tile, x_refs[0][...], x_refs[1][...]) if two_in else x_refs[0][...]
    xn = _rms(x, g_ref[...]).astype(BF16)
    gate = jnp.dot(xn, wg_ref[...], preferred_element_type=F32)
    up = jnp.dot(xn, wu_ref[...], preferred_element_type=F32)
    h = (_silu(gate) * up).astype(BF16)
    y = x + 0.5 * jnp.dot(h, wd_ref[...], preferred_element_type=F32)
    if final_norm:
        y = _rms(y, fg_ref[...])
    if two_out:
        @pl.when(i < split_tile)
        def _():
            o_refs[0][...] = y

        @pl.when(i >= split_tile)
        def _():
            o_refs[1][...] = y
    else:
        o_refs[0][...] = y


def _inproj_kernel(x_ref, g_ref, w_ref, o_ref):
    xn = _rms(x_ref[...], g_ref[...]).astype(BF16)
    o_ref[...] = jnp.dot(xn, w_ref[...], preferred_element_type=F32)


def _outproj_kernel(x_ref, m_ref, w_ref, o_ref):
    o_ref[...] = x_ref[...] + jnp.dot(m_ref[...], w_ref[...], preferred_element_type=F32)


def _resident(shape):
    nd = len(shape)
    return pl.BlockSpec(shape, lambda *_: (0,) * nd, pipeline_mode=pl.Buffered(1))


def _dense_params():
    return pltpu.CompilerParams(dimension_semantics=("parallel",), vmem_limit_bytes=VMEM_LIMIT_BYTES)


def _ffn(xs, gain, wg, wu, wd, final_gain=None, split_rows=None):
    t = sum(x.shape[0] for x in xs)
    two_in = len(xs) == 2
    two_out = final_gain is not None and split_rows is not None
    split_tile = None
    if two_in or two_out:
        split_tile = (xs[0].shape[0] if two_in else split_rows) // DENSE_TILE
    tile = pl.BlockSpec((DENSE_TILE, D_MODEL), lambda i: (i, 0))
    first = pl.BlockSpec((DENSE_TILE, D_MODEL), lambda i: (jnp.minimum(i, split_tile - 1), 0))
    second = pl.BlockSpec((DENSE_TILE, D_MODEL), lambda i: (jnp.maximum(i - split_tile, 0), 0))
    in_specs = ([first, second] if two_in else [tile]) + [
        _resident((1, D_MODEL)), _resident(wg.shape), _resident(wu.shape), _resident(wd.shape)]
    args = list(xs) + [gain.reshape(1, D_MODEL), wg, wu, wd]
    if final_gain is not None:
        in_specs.append(_resident((1, D_MODEL)))
        args.append(final_gain.reshape(1, D_MODEL))
    if two_out:
        out_specs = [first, second]
        out_shape = [jax.ShapeDtypeStruct((split_rows, D_MODEL), F32),
                     jax.ShapeDtypeStruct((t - split_rows, D_MODEL), F32)]
    else:
        out_specs, out_shape = tile, jax.ShapeDtypeStruct((t, D_MODEL), F32)
    return pl.pallas_call(
        functools.partial(_ffn_kernel, split_tile, final_gain is not None),
        grid=(t // DENSE_TILE,), in_specs=in_specs, out_specs=out_specs, out_shape=out_shape,
        compiler_params=pltpu.CompilerParams(dimension_semantics=("arbitrary",),
                                             vmem_limit_bytes=VMEM_LIMIT_BYTES),
        name="ffn" if final_gain is None else "ffn_final_norm")(*args)


def _inproj(x, gain, w):
    t = x.shape[0]
    return pl.pallas_call(
        _inproj_kernel, grid=(t // DENSE_TILE,),
        in_specs=[pl.BlockSpec((DENSE_TILE, D_MODEL), lambda i: (i, 0)), _resident((1, D_MODEL)), _resident(w.shape)],
        out_specs=pl.BlockSpec((DENSE_TILE, PROJ_COLS), lambda i: (i, 0)),
        out_shape=jax.ShapeDtypeStruct((t, PROJ_COLS), F32), compiler_params=_dense_params(),
        name="inproj")(x, gain.reshape(1, D_MODEL), w)


def _outproj(x, mix, w):
    t = x.shape[0]
    return pl.pallas_call(
        _outproj_kernel, grid=(t // DENSE_TILE,),
        in_specs=[pl.BlockSpec((DENSE_TILE, D_MODEL), lambda i: (i, 0)),
                  pl.BlockSpec((DENSE_TILE, D_MIX), lambda i: (i, 0)), _resident(w.shape)],
        out_specs=pl.BlockSpec((DENSE_TILE, D_MODEL), lambda i: (i, 0)),
        out_shape=jax.ShapeDtypeStruct((t, D_MODEL), F32), compiler_params=_dense_params(),
        name="outproj")(x, mix, w)


def _mixer_constants(nseq, seq_len):
    c = nseq * seq_len
    idx = np.arange(c)
    seq, pos = idx // seq_len, idx % seq_len
    same = seq[:, None] == seq[None, :]
    pi, pj = pos[:, None], pos[None, :]
    causal = same & (pi >= pj)
    base = min(seq_len, NEUMANN_BLOCK)
    merges = []
    s = base
    while s < seq_len:
        merges.append(same & (pi // (2 * s) == pj // (2 * s)) & (pi // s != pj // s))
        s *= 2
    levels = []
    m = seq_len // 2
    while m >= 1:
        levels.append(m)
        m //= 2
    sel = np.zeros((len(levels), c, c), np.float32)
    lv = np.full((c, c), -1, np.int32)
    xor = pi ^ pj
    for k, m in enumerate(levels):
        mid = seq * seq_len + (pos // (2 * m)) * (2 * m) + (m - 1)
        sel[k, idx, mid] = 1.0
        lv[same & (pi > pj) & (xor >= m) & (xor < 2 * m)] = k
    lv[idx, idx] = len(levels)
    expand = np.zeros((LANES, M2_DIM), np.float32)
    for h in range(M2_HEADS):
        expand[SM_MDT + h, h * M2_HEADDIM:(h + 1) * M2_HEADDIM] = 1.0
    consts = dict(
        tril=jnp.asarray(causal, BF16),
        negi=jnp.asarray(np.where(causal, 0.0, MASKED), F32),
        blk=jnp.asarray(same & (pi // base == pj // base), F32),
        merge=jnp.asarray(np.stack(merges) if merges else np.zeros((1, SUBLANES, LANES)), F32),
        sel=jnp.asarray(sel.reshape(len(levels) * c, c), BF16),
        lv=jnp.asarray(lv),
        expand=jnp.asarray(expand, BF16),
    )
    nsquare = int(math.log2(base)) - 1
    return consts, len(levels), len(merges), nsquare


def _mixer_kernel(nseq, seq_len, nlev, nmerge, nsquare, single_chunk, n_alias,
                  proj_ref, conv0_ref, dn0_ref, gla0_ref, ssm0_ref,
                  convw_ref, convb_ref, alog_ref, dtb_ref, dn_norm_ref, gla_wup_ref, gla_bup_ref, gla_norm_ref,
                  m2_d_ref, m2_norm_ref,
                  tril_ref, negi_ref, blk_ref, merge_ref, sel_ref, lv_ref, expand_ref, *rest):
    mix_ref, convn_ref, dnn_ref, glan_ref, ssmn_ref, cbuf, spair = rest[n_alias:]
    c = nseq * seq_len
    chunk = pl.program_id(1)
    last_chunk = pl.num_programs(1) - 1
    seq_rows = [slice(b * seq_len, (b + 1) * seq_len) for b in range(nseq)]

    def at_first_chunk(fn):
        if single_chunk:
            fn()
        else:
            pl.when(chunk == 0)(fn)

    def at_last_chunk(fn):
        if single_chunk:
            fn()
        else:
            pl.when(chunk == last_chunk)(fn)

    @at_first_chunk
    def _():
        cbuf[:, pl.ds(CONV_PAD - CONV_TAIL, CONV_TAIL), :] = conv0_ref[...]
        if not single_chunk:
            dnn_ref[...] = dn0_ref[...]
            glan_ref[...] = gla0_ref[...]

    cbuf[:, pl.ds(CONV_PAD, seq_len), :] = proj_ref[:, 0:CONV_CH].reshape(nseq, seq_len, CONV_CH)

    def conv(lo, width):
        acc = convb_ref[:, lo:lo + width].reshape(1, 1, width)
        for w in range(CONV_W):
            tap = convw_ref[w:w + 1, lo:lo + width].reshape(1, 1, width)
            acc = acc + tap * cbuf[:, pl.ds(CONV_PAD - CONV_TAIL + w, seq_len), lo:lo + width]
        return _silu(acc).reshape(c, width)

    ones = jnp.ones((LANES, LANES), BF16)

    def row_sum(x):
        return jnp.dot(x.astype(BF16), ones, preferred_element_type=F32)

    def per_seq_last(x):
        parts = [jnp.broadcast_to(x[r.stop - 1:r.stop, :], (seq_len, x.shape[1])) for r in seq_rows]
        return parts[0] if nseq == 1 else jnp.concatenate(parts, axis=0)

    def cat_rows(parts):
        return parts[0] if nseq == 1 else jnp.concatenate(parts, axis=0)

    tril = tril_ref[...]
    negi = negi_ref[...]
    lv = lv_ref[...]
    diag = lv == nlev

    small = proj_ref[:, OFF_SMALL:OFF_SMALL + LANES]
    sp = _softplus(small + dtb_ref[...])
    g2 = (-LOG2E * jnp.exp(alog_ref[...])) * sp
    cum = _dot_exact_lhs(tril, g2)
    cum_t = cum.T
    last = per_seq_last(cum)
    e_cum = jnp.exp2(cum)
    e_rem = jnp.exp2(last - cum)
    e_last = jnp.exp2(last)
    beta_all = jax.nn.sigmoid(small)

    def decay_matrix(lane):
        return jnp.exp2(cum[:, lane:lane + 1] - cum_t[lane:lane + 1, :] + negi)

    dq_all = conv(OFF_DQ, DN_QK)
    dk_all = conv(OFF_DK, DN_QK)
    dv_all = conv(OFF_DV, DN_V)
    dn = []
    for h in range(DN_HEADS):
        hs = slice(h * DN_DK, (h + 1) * DN_DK)
        dq, dk = dq_all[:, hs], dk_all[:, hs]
        q = dq * (lax.rsqrt(row_sum(dq * dq) + EPS) * (DN_DK ** -0.5))
        k = dk * lax.rsqrt(row_sum(dk * dk) + EPS)
        kb, qb = k.astype(BF16), q.astype(BF16)
        decay = decay_matrix(SM_DA + h)
        beta = beta_all[:, SM_DB + h:SM_DB + h + 1]
        kk = lax.dot_general(kb, kb, (((1,), (1,)), ((), ())), preferred_element_type=F32)
        qk = lax.dot_general(qb, kb, (((1,), (1,)), ((), ())), preferred_element_type=F32)
        a_mat = jnp.where(diag, 0.0, kk * decay * beta)
        dn.append(dict(q=q, k=k, v=dv_all[:, hs], beta=beta, a=a_mat, qkd=qk * decay))

    blk = blk_ref[...]
    for st in dn:
        st["p"] = st["a"] * blk
        st["n"] = -st["p"]
    for _ in range(nsquare):
        for st in dn:
            st["p"] = _dot(st["p"], st["p"])
        for st in dn:
            st["n"] = st["n"] + st["p"] + _dot(st["n"], st["p"])
    for lvl in range(nmerge):
        emask = merge_ref[lvl]
        for st in dn:
            e = st["a"] * emask
            st["y"] = e + _dot(e, st["n"])
        for st in dn:
            st["n"] = st["n"] - st["y"] - _dot(st["n"], st["y"])

    for h, st in enumerate(dn):
        q, k, beta = st["q"], st["k"], st["beta"]
        lane = SM_DA + h
        egc = e_cum[:, lane:lane + 1]
        rhs = jnp.concatenate([k * (beta * egc), st["v"] * beta], axis=1)
        sol = rhs + _dot(st["n"], rhs)
        w_mat, u_mat = sol[:, :DN_DK], sol[:, DN_DK:]
        qe = q * egc
        v_parts, o_parts, states = [], [], []
        for b, r in enumerate(seq_rows):
            s = dn0_ref[b, h] if single_chunk else dnn_ref[b, h]
            both = _dot(jnp.concatenate([w_mat[r], qe[r]], axis=0), s)
            v_parts.append(u_mat[r] - both[:seq_len])
            o_parts.append(both[seq_len:])
            states.append(s)
        v_new = cat_rows(v_parts)
        o = cat_rows(o_parts) + _dot(st["qkd"], v_new)
        kdec = k * e_rem[:, lane:lane + 1]
        for b, r in enumerate(seq_rows):
            dnn_ref[b, h] = e_last[r.start:r.start + 1, lane:lane + 1] * states[b] + _dot_tn(kdec[r], v_new[r])
        z = proj_ref[:, OFF_DZ + h * DN_DV:OFF_DZ + (h + 1) * DN_DV]
        o_n = o * lax.rsqrt(row_sum(o * o) * (1.0 / DN_DV) + EPS) * dn_norm_ref[...]
        mix_ref[:, h * DN_DV:(h + 1) * DN_DV] = (o_n * _silu(z)).astype(BF16)

    glow = small[:, SM_GLOW:SM_GLOW + GLA_RANK]
    zg = _dot(glow, gla_wup_ref[...]) + gla_bup_ref[...]
    gk2 = _softplus(-zg) * (-LOG2E / GLA_NORMALIZER)
    gcum = _dot_exact_lhs(tril, gk2)
    gq = proj_ref[:, OFF_GQ:OFF_GQ + GLA_QK] * (GLA_DK ** -0.5)
    gkk = proj_ref[:, OFF_GK:OFF_GK + GLA_QK]
    gq_b, gkk_b = gq.astype(BF16), gkk.astype(BF16)
    gmid_all = _dot_exact_lhs(sel_ref[...], gcum)
    level_ops = []
    for lev in range(nlev):
        e = jnp.exp2(-jnp.abs(gcum - gmid_all[lev * c:(lev + 1) * c, :]))
        level_ops.append(((gq * e).astype(BF16), (gkk * e).astype(BF16), lv == lev))
    glast = per_seq_last(gcum)
    q_inter = gq * jnp.exp2(gcum)
    k_dec = gkk * jnp.exp2(glast - gcum)
    glast_cols = [gcum[r.stop - SUBLANES:r.stop, :].T[:, SUBLANES - 1:SUBLANES] for r in seq_rows]
    for h in range(GLA_HEADS):
        hs = slice(h * GLA_DK, (h + 1) * GLA_DK)
        v = proj_ref[:, OFF_GV + h * GLA_DV:OFF_GV + (h + 1) * GLA_DV]
        scores = jnp.where(diag, lax.dot_general(gq_b[:, hs], gkk_b[:, hs], (((1,), (1,)), ((), ())),
                                                 preferred_element_type=F32), 0.0)
        for q_t, k_t, mask in level_ops:
            s_lev = lax.dot_general(q_t[:, hs], k_t[:, hs], (((1,), (1,)), ((), ())), preferred_element_type=F32)
            scores = jnp.where(mask, s_lev, scores)
        o_parts = []
        for b, r in enumerate(seq_rows):
            s = gla0_ref[b, h] if single_chunk else glan_ref[b, h]
            o_parts.append(_dot(q_inter[r, hs], s))
            glan_ref[b, h] = jnp.exp2(glast_cols[b][hs, :]) * s + _dot_tn(k_dec[r, hs], v[r])
        o = cat_rows(o_parts) + _dot(scores, v)
        z = proj_ref[:, OFF_GG + h * GLA_DV:OFF_GG + (h + 1) * GLA_DV]
        o_n = o * lax.rsqrt(row_sum(o * o) * (1.0 / GLA_DV) + EPS) * gla_norm_ref[...]
        mix_ref[:, DN_V + h * GLA_DV:DN_V + (h + 1) * GLA_DV] = (o_n * _silu(z)).astype(BF16)

    expand = expand_ref[...]
    dt_x = _dot_exact_rhs(sp, expand)
    ecum_x = _dot_exact_rhs(e_cum, expand)
    dtrem_x = _dot_exact_rhs(sp * e_rem, expand)
    last8 = cat_rows([e_last[r.start:r.start + SUBLANES, :] for r in seq_rows])
    elast_x = _dot_exact_rhs(last8, expand)
    lane_lo = lax.broadcasted_iota(jnp.int32, (c, LANES), 1) < M2_HEADDIM

    @at_first_chunk
    def _():
        for b in range(nseq):
            for p in range(M2_PAIRS):
                spair[b, p] = jnp.concatenate([ssm0_ref[b, 2 * p], ssm0_ref[b, 2 * p + 1]], axis=1)

    m2_off = DN_V + GLA_V
    total = jnp.zeros((c, LANES), F32)
    gated = []
    for g in range(M2_GROUPS):
        bm = conv(OFF_MB + g * M2_STATE, M2_STATE)
        cm = conv(OFF_MC + g * M2_STATE, M2_STATE)
        bm_b, cm_b = bm.astype(BF16), cm.astype(BF16)
        cb = lax.dot_general(cm_b, bm_b, (((1,), (1,)), ((), ())), preferred_element_type=F32)
        for pp in range(M2_PAIRS_PER_GROUP):
            p = g * M2_PAIRS_PER_GROUP + pp
            ps = slice(p * LANES, (p + 1) * LANES)
            xh = conv(OFF_MX + p * LANES, LANES)
            x_dt = xh * dt_x[:, ps]
            m_pair = jnp.concatenate([cb * decay_matrix(SM_MDT + 2 * p), cb * decay_matrix(SM_MDT + 2 * p + 1)],
                                     axis=1)
            x_bd = jnp.concatenate([jnp.where(lane_lo, x_dt, 0.0), jnp.where(lane_lo, 0.0, x_dt)], axis=0)
            x_rem = (xh * dtrem_x[:, ps]).astype(BF16)
            inter = []
            for b, r in enumerate(seq_rows):
                s = spair[b, p]
                inter.append(jnp.dot(cm_b[r], s.astype(BF16), preferred_element_type=F32))
                spair[b, p] = (elast_x[b * SUBLANES:b * SUBLANES + 1, ps] * s
                               + lax.dot_general(bm_b[r], x_rem[r], (((0,), (0,)), ((), ())),
                                                 preferred_element_type=F32))
            y = _dot(m_pair, x_bd) + cat_rows(inter) * ecum_x[:, ps] + m2_d_ref[:, ps] * xh
            z = proj_ref[:, OFF_MZ + p * LANES:OFF_MZ + (p + 1) * LANES]
            yz = y * _silu(z)
            total = total + yz * yz
            gated.append(yz)
    scale = lax.rsqrt(row_sum(total) * (1.0 / M2_DIM) + EPS)
    for p, yz in enumerate(gated):
        ps = slice(p * LANES, (p + 1) * LANES)
        mix_ref[:, m2_off + p * LANES:m2_off + (p + 1) * LANES] = (yz * scale * m2_norm_ref[:, ps]).astype(BF16)

    @at_last_chunk
    def _():
        for b in range(nseq):
            for p in range(M2_PAIRS):
                s = spair[b, p]
                ssmn_ref[b, 2 * p] = s[:, :M2_HEADDIM]
                ssmn_ref[b, 2 * p + 1] = s[:, M2_HEADDIM:]
        convn_ref[...] = cbuf[:, pl.ds(CONV_PAD + seq_len - CONV_TAIL, CONV_TAIL), :]

    if not single_chunk:
        cbuf[:, pl.ds(0, CONV_PAD), :] = cbuf[:, pl.ds(seq_len, CONV_PAD), :]


def _mixer(proj, row_start, nb, seq, nseq, seq_len, layer, conv0, dn0, gla0, ssm0, params,
           total_rows, prev_mix=None, prev_states=None):
    c = nseq * seq_len
    nc = seq // seq_len
    blk0 = row_start // c
    lsel = layer if conv0.shape[0] > 1 else 0
    consts, nlev, nmerge, nsquare = _mixer_constants(nseq, seq_len)
    const_args = [consts[k] for k in ("tril", "negi", "blk", "merge", "sel", "lv", "expand")]

    def state_spec(shape):
        zeros = (0,) * len(shape)
        return pl.BlockSpec((None, nseq) + shape, lambda b, k: (lsel, b) + zeros)

    def out_state_spec(shape):
        zeros = (0,) * len(shape)
        return pl.BlockSpec((None, nseq) + shape, lambda b, k: (layer, b) + zeros)

    conv_shape = (CONV_TAIL, CONV_CH)
    dn_shape = (DN_HEADS, DN_DK, DN_DV)
    gla_shape = (GLA_HEADS, GLA_DK, GLA_DV)
    ssm_shape = (M2_HEADS, M2_STATE, M2_HEADDIM)
    state_shapes = (conv_shape, dn_shape, gla_shape, ssm_shape)
    in_specs = [pl.BlockSpec((c, PROJ_COLS), lambda b, k: (blk0 + b * nc + k, 0))]
    in_specs += [state_spec(s) for s in state_shapes]
    in_specs += [_resident(p.shape) for p in params]
    in_specs += [_resident(a.shape) for a in const_args]
    args = [proj, conv0, dn0, gla0, ssm0, *params, *const_args]
    aliases = {}
    if prev_mix is not None:
        aliases[len(args)] = 0
        args.append(prev_mix)
    if prev_states is not None:
        for i, st in enumerate(prev_states):
            aliases[len(args)] = 1 + i
            args.append(st)
    n_alias = len(aliases)
    in_specs += [pl.BlockSpec(memory_space=pl.ANY)] * n_alias
    out_specs = [pl.BlockSpec((c, D_MIX), lambda b, k: (blk0 + b * nc + k, 0))]
    out_specs += [out_state_spec(s) for s in state_shapes]
    out_shape = [jax.ShapeDtypeStruct((total_rows, D_MIX), BF16)]
    out_shape += [jax.ShapeDtypeStruct((DEPTH, nb) + s, F32) for s in state_shapes]
    body = functools.partial(_mixer_kernel, nseq, seq_len, nlev, nmerge, nsquare, nc == 1, n_alias)
    return pl.pallas_call(
        body, grid=(nb // nseq, nc), in_specs=in_specs, out_specs=out_specs, out_shape=out_shape,
        input_output_aliases=aliases,
        scratch_shapes=[pltpu.VMEM((nseq, CONV_PAD + seq_len, CONV_CH), F32),
                        pltpu.VMEM((nseq, M2_PAIRS, M2_STATE, LANES), F32)],
        compiler_params=pltpu.CompilerParams(dimension_semantics=("parallel", "arbitrary"),
                                             vmem_limit_bytes=VMEM_LIMIT_BYTES),
        name=f"mixer_{nseq}x{seq_len}")(*args)


def _permute_w_in(w):
    o = CONV_CH
    dz = w[:, o:o + DN_V]; o += DN_V
    da = w[:, o:o + DN_HEADS]; o += DN_HEADS
    db = w[:, o:o + DN_HEADS]; o += DN_HEADS
    gq = w[:, o:o + GLA_QK]; o += GLA_QK
    gk = w[:, o:o + GLA_QK]; o += GLA_QK
    gv = w[:, o:o + GLA_V]; o += GLA_V
    glow = w[:, o:o + GLA_RANK]; o += GLA_RANK
    gg = w[:, o:o + GLA_V]; o += GLA_V
    mz = w[:, o:o + M2_DIM]; o += M2_DIM
    mdt = w[:, o:o + M2_HEADS]; o += M2_HEADS
    assert o == w.shape[1]
    pad = jnp.zeros((w.shape[0], LANES - SM_USED), w.dtype)
    return jnp.concatenate([w[:, :CONV_CH], dz, gq, gk, gv, gg, mz, da, mdt, db, glow, pad], axis=1)


def _small_lane_row(dn_vals, m2_vals):
    pad = jnp.zeros((LANES - SM_DB,), F32)
    return jnp.concatenate([dn_vals.astype(F32), m2_vals.astype(F32), pad]).reshape(1, LANES)


def kernel(x_prompt, x_sample, state_conv, state_delta, state_gla, state_ssm, ffn1_norm, ffn1_w_gu, ffn1_w_down, mix_norm, w_in, conv_w, conv_b, dn_a_log, dn_dt_bias, dn_norm, gla_w_up, gla_b_up, gla_norm, m2_a_log, m2_dt_bias, m2_d, m2_norm, w_out, ffn2_norm, ffn2_w_gu, ffn2_w_down, final_norm):
    bp, sp, _ = x_prompt.shape
    bs, ss, _ = x_sample.shape
    tp, ts = bp * sp, bs * ss
    assert tp % DENSE_TILE == 0 and ts % DENSE_TILE == 0 and sp % PROMPT_CHUNK == 0
    assert bs % SAMPLE_SEQS_PER_STEP == 0 and tp % (SAMPLE_SEQS_PER_STEP * ss) == 0 and ss % SUBLANES == 0

    zero_conv = jnp.zeros((1, bp, CONV_TAIL, CONV_CH), F32)
    zero_dn = jnp.zeros((1, bp, DN_HEADS, DN_DK, DN_DV), F32)
    zero_gla = jnp.zeros((1, bp, GLA_HEADS, GLA_DK, GLA_DV), F32)
    zero_ssm = jnp.zeros((1, bp, M2_HEADS, M2_STATE, M2_HEADDIM), F32)

    xs = [x_prompt.reshape(tp, D_MODEL), x_sample.reshape(ts, D_MODEL)]
    st_p = st_s = None
    for l in range(DEPTH):
        wg1, wu1 = ffn1_w_gu[l, :, :D_FF].astype(BF16), ffn1_w_gu[l, :, D_FF:].astype(BF16)
        wg2, wu2 = ffn2_w_gu[l, :, :D_FF].astype(BF16), ffn2_w_gu[l, :, D_FF:].astype(BF16)
        params = (conv_w[l], conv_b[l].reshape(1, CONV_CH),
                  _small_lane_row(dn_a_log[l], m2_a_log[l]), _small_lane_row(dn_dt_bias[l], m2_dt_bias[l]),
                  dn_norm[l].reshape(1, DN_DV),
                  gla_w_up[l].astype(BF16), gla_b_up[l].reshape(1, GLA_QK), gla_norm[l].reshape(1, GLA_DV),
                  jnp.repeat(m2_d[l], M2_HEADDIM).reshape(1, M2_DIM), m2_norm[l].reshape(1, M2_DIM))

        x = _ffn(xs, ffn1_norm[l], wg1, wu1, ffn1_w_down[l].astype(BF16))
        proj = _inproj(x, mix_norm[l], _permute_w_in(w_in[l]).astype(BF16))
        mix, *st_p = _mixer(proj, 0, bp, sp, 1, PROMPT_CHUNK, l, zero_conv, zero_dn, zero_gla, zero_ssm, params,
                            tp + ts, prev_states=st_p)
        mix, *st_s = _mixer(proj, tp, bs, ss, SAMPLE_SEQS_PER_STEP, ss, l,
                            state_conv, state_delta, state_gla, state_ssm, params,
                            tp + ts, prev_mix=mix, prev_states=st_s)
        x = _outproj(x, mix, w_out[l].astype(BF16))
        if l < DEPTH - 1:
            xs = [_ffn([x], ffn2_norm[l], wg2, wu2, ffn2_w_down[l].astype(BF16))]
        else:
            y_prompt, y_sample = _ffn([x], ffn2_norm[l], wg2, wu2, ffn2_w_down[l].astype(BF16),
                                      final_gain=final_norm, split_rows=tp)

    return (y_prompt.reshape(bp, sp, D_MODEL), y_sample.reshape(bs, ss, D_MODEL), *st_p, *st_s)
```

```python
import functools
import math

import jax
import jax.numpy as jnp
import numpy as np
from jax import lax
from jax.experimental import pallas as pl
from jax.experimental.pallas import tpu as pltpu

F32 = jnp.float32
BF16 = jnp.bfloat16
LOG2E = math.log2(math.e)

D_MODEL = 1024
DEPTH = 2
D_MIX = 2 * D_MODEL
D_FF = 2816
CONV_W = 4
DN_HEADS = 4
DN_DK = 128
DN_DV = 128
GLA_HEADS = 4
GLA_DK = 64
GLA_DV = 128
GLA_RANK = 16
GLA_NORMALIZER = 16.0
M2_DIM = 1024
M2_HEADDIM = 64
M2_HEADS = 16
M2_STATE = 128
M2_GROUPS = 2
M2_PAIRS = M2_HEADS // 2
M2_PAIRS_PER_GROUP = M2_PAIRS // M2_GROUPS
EPS = 1e-6
LANES = 128
SUBLANES = 8
SUBLANES_BF16 = 16

DN_QK = DN_HEADS * DN_DK
DN_V = DN_HEADS * DN_DV
GLA_QK = GLA_HEADS * GLA_DK
GLA_V = GLA_HEADS * GLA_DV
M2_BC = M2_GROUPS * M2_STATE
CONV_CH = 3 * 512 + M2_DIM + 2 * M2_BC

OFF_DQ = 0
OFF_DK = OFF_DQ + DN_QK
OFF_DV = OFF_DK + DN_QK
OFF_MX = OFF_DV + DN_V
OFF_MB = OFF_MX + M2_DIM
OFF_MC = OFF_MB + M2_BC
OFF_DZ = CONV_CH
OFF_GQ = OFF_DZ + DN_V
OFF_GK = OFF_GQ + GLA_QK
OFF_GV = OFF_GK + GLA_QK
OFF_GG = OFF_GV + GLA_V
OFF_MZ = OFF_GG + GLA_V
OFF_SMALL = OFF_MZ + M2_DIM
SM_DA = 0
SM_MDT = SM_DA + DN_HEADS
SM_DB = SM_MDT + M2_HEADS
SM_GLOW = SM_DB + DN_HEADS
SM_USED = SM_GLOW + GLA_RANK
PROJ_COLS = OFF_SMALL + LANES

CONV_PAD = SUBLANES
CONV_TAIL = CONV_W - 1
NEUMANN_BLOCK = 16
VMEM_LIMIT_BYTES = 56 * 1024 * 1024
PROMPT_CHUNK = 128
SAMPLE_SEQS_PER_STEP = 8
DENSE_TILE = 256
MASKED = -1e30
NT_DIMS = (((1,), (1,)), ((), ()))
TN_DIMS = (((0,), (0,)), ((), ()))


def _dot(a, b):
    return jnp.dot(a.astype(BF16), b.astype(BF16), preferred_element_type=F32)


def _dot_nt(a, b):
    return lax.dot_general(a.astype(BF16), b.astype(BF16), NT_DIMS, preferred_element_type=F32)


def _dot_tn(a, b):
    return lax.dot_general(a.astype(BF16), b.astype(BF16), TN_DIMS, preferred_element_type=F32)


def _split3(x):
    hi = x.astype(BF16)
    r1 = x - hi.astype(F32)
    mid = r1.astype(BF16)
    lo = (r1 - mid.astype(F32)).astype(BF16)
    return hi, mid, lo


def _sum_dots(pieces, fn):
    acc = None
    for piece in pieces:
        part = fn(piece)
        acc = part if acc is None else acc + part
    return acc


def _dot_exact_lhs(m, pieces):
    return _sum_dots(pieces, lambda p: jnp.dot(m, p, preferred_element_type=F32))


def _dot_exact_rhs(pieces, m):
    return _sum_dots(pieces, lambda p: jnp.dot(p, m, preferred_element_type=F32))


def _dot_exact_tn(pieces, m):
    return _sum_dots(pieces, lambda p: lax.dot_general(p, m, TN_DIMS, preferred_element_type=F32))


def _silu(x):
    return x * jax.nn.sigmoid(x)


def _softplus(x):
    return jnp.maximum(x, 0.0) + jnp.log(1.0 + jnp.exp(-jnp.abs(x)))


def _rms(x, gain):
    return x * lax.rsqrt(jnp.mean(x * x, axis=-1, keepdims=True) + EPS) * gain


def _ffn_kernel(split_tile, final_norm, *refs):
    refs = list(refs)
    two_in = split_tile is not None and not final_norm
    two_out = split_tile is not None and final_norm
    x_refs = [refs.pop(0) for _ in range(2 if two_in else 1)]
    g_ref, wg_ref, wu_ref, wd_ref = (refs.pop(0) for _ in range(4))
    fg_ref = refs.pop(0) if final_norm else None
    o_refs = refs
    i = pl.program_id(0)
    x = jnp.where(i < split_tile, x_refs[0][...], x_refs[1][...]) if two_in else x_refs[0][...]
    xn = _rms(x, g_ref[...]).astype(BF16)
    gate = jnp.dot(xn, wg_ref[...], preferred_element_type=F32)
    up = jnp.dot(xn, wu_ref[...], preferred_element_type=F32)
    h = (_silu(gate) * up).astype(BF16)
    y = x + 0.5 * jnp.dot(h, wd_ref[...], preferred_element_type=F32)
    if final_norm:
        y = _rms(y, fg_ref[...])
    if two_out:
        @pl.when(i < split_tile)
        def _():
            o_refs[0][...] = y

        @pl.when(i >= split_tile)
        def _():
            o_refs[1][...] = y
    else:
        o_refs[0][...] = y


def _inproj_kernel(x_ref, g_ref, w_ref, o_ref):
    xn = _rms(x_ref[...], g_ref[...]).astype(BF16)
    o_ref[...] = jnp.dot(xn, w_ref[...], preferred_element_type=F32)


def _outproj_kernel(x_ref, m_ref, w_ref, o_ref):
    o_ref[...] = x_ref[...] + jnp.dot(m_ref[...], w_ref[...], preferred_element_type=F32)


def _resident(shape):
    nd = len(shape)
    return pl.BlockSpec(shape, lambda *_: (0,) * nd, pipeline_mode=pl.Buffered(1))


def _dense_params():
    return pltpu.CompilerParams(dimension_semantics=("parallel",), vmem_limit_bytes=VMEM_LIMIT_BYTES)


def _ffn(xs, gain, wg, wu, wd, final_gain=None, split_rows=None):
    t = sum(x.shape[0] for x in xs)
    two_in = len(xs) == 2
    two_out = final_gain is not None and split_rows is not None
    split_tile = None
    if two_in or two_out:
        split_tile = (xs[0].shape[0] if two_in else split_rows) // DENSE_TILE
    tile = pl.BlockSpec((DENSE_TILE, D_MODEL), lambda i: (i, 0))
    first = pl.BlockSpec((DENSE_TILE, D_MODEL), lambda i: (jnp.minimum(i, split_tile - 1), 0))
    second = pl.BlockSpec((DENSE_TILE, D_MODEL), lambda i: (jnp.maximum(i - split_tile, 0), 0))
    in_specs = ([first, second] if two_in else [tile]) + [
        _resident((1, D_MODEL)), _resident(wg.shape), _resident(wu.shape), _resident(wd.shape)]
    args = list(xs) + [gain.reshape(1, D_MODEL), wg, wu, wd]
    if final_gain is not None:
        in_specs.append(_resident((1, D_MODEL)))
        args.append(final_gain.reshape(1, D_MODEL))
    if two_out:
        out_specs = [first, second]
        out_shape = [jax.ShapeDtypeStruct((split_rows, D_MODEL), F32),
                     jax.ShapeDtypeStruct((t - split_rows, D_MODEL), F32)]
    else:
        out_specs, out_shape = tile, jax.ShapeDtypeStruct((t, D_MODEL), F32)
    return pl.pallas_call(
        functools.partial(_ffn_kernel, split_tile, final_gain is not None),
        grid=(t // DENSE_TILE,), in_specs=in_specs, out_specs=out_specs, out_shape=out_shape,
        compiler_params=pltpu.CompilerParams(dimension_semantics=("arbitrary",),
                                             vmem_limit_bytes=VMEM_LIMIT_BYTES),
        name="ffn" if final_gain is None else "ffn_final_norm")(*args)


def _inproj(x, gain, w):
    t = x.shape[0]
    return pl.pallas_call(
        _inproj_kernel, grid=(t // DENSE_TILE,),
        in_specs=[pl.BlockSpec((DENSE_TILE, D_MODEL), lambda i: (i, 0)), _resident((1, D_MODEL)), _resident(w.shape)],
        out_specs=pl.BlockSpec((DENSE_TILE, PROJ_COLS), lambda i: (i, 0)),
        out_shape=jax.ShapeDtypeStruct((t, PROJ_COLS), F32), compiler_params=_dense_params(),
        name="inproj")(x, gain.reshape(1, D_MODEL), w)


def _outproj(x, mix, w):
    t = x.shape[0]
    return pl.pallas_call(
        _outproj_kernel, grid=(t // DENSE_TILE,),
        in_specs=[pl.BlockSpec((DENSE_TILE, D_MODEL), lambda i: (i, 0)),
                  pl.BlockSpec((DENSE_TILE, D_MIX), lambda i: (i, 0)), _resident(w.shape)],
        out_specs=pl.BlockSpec((DENSE_TILE, D_MODEL), lambda i: (i, 0)),
        out_shape=jax.ShapeDtypeStruct((t, D_MODEL), F32), compiler_params=_dense_params(),
        name="outproj")(x, mix, w)


CONST_NAMES = ("tril", "negi", "blk", "merge", "sel", "lv", "expand", "bc_dn", "bc_beta", "bc_ssd",
               "first_row", "last_row")


def _mixer_constants(nseq, seq_len):
    c = nseq * seq_len
    idx = np.arange(c)
    seq, pos = idx // seq_len, idx % seq_len
    same = seq[:, None] == seq[None, :]
    pi, pj = pos[:, None], pos[None, :]
    causal = same & (pi >= pj)
    base = min(seq_len, NEUMANN_BLOCK)
    merges = []
    s = base
    while s < seq_len:
        merges.append(same & (pi // (2 * s) == pj // (2 * s)) & (pi // s != pj // s))
        s *= 2
    levels = []
    m = seq_len // 2
    while m >= 1:
        levels.append(m)
        m //= 2
    sel = np.zeros((len(levels), c, c), np.float32)
    lv = np.full((c, c), -1, np.int32)
    xor = pi ^ pj
    for k, m in enumerate(levels):
        mid = seq * seq_len + (pos // (2 * m)) * (2 * m) + (m - 1)
        sel[k, idx, mid] = 1.0
        lv[same & (pi > pj) & (xor >= m) & (xor < 2 * m)] = k
    lv[idx, idx] = len(levels)
    expand = np.zeros((LANES, M2_DIM), np.float32)
    bc_ssd = np.zeros((LANES, M2_HEADS * LANES), np.float32)
    for h in range(M2_HEADS):
        expand[SM_MDT + h, h * M2_HEADDIM:(h + 1) * M2_HEADDIM] = 1.0
        bc_ssd[SM_MDT + h, h * LANES:(h + 1) * LANES] = 1.0
    bc_dn = np.zeros((LANES, DN_HEADS * LANES), np.float32)
    bc_beta = np.zeros((LANES, DN_HEADS * LANES), np.float32)
    for h in range(DN_HEADS):
        bc_dn[SM_DA + h, h * LANES:(h + 1) * LANES] = 1.0
        bc_beta[SM_DB + h, h * LANES:(h + 1) * LANES] = 1.0
    first_row = np.zeros((SUBLANES_BF16, LANES), np.float32)
    first_row[0] = 1.0
    last_row = np.zeros((c, nseq * LANES), np.float32)
    for b in range(nseq):
        last_row[(b + 1) * seq_len - 1, b * LANES:(b + 1) * LANES] = 1.0
    consts = dict(
        tril=jnp.asarray(causal, BF16),
        negi=jnp.asarray(np.where(causal, 0.0, MASKED), F32),
        blk=jnp.asarray(same & (pi // base == pj // base), F32),
        merge=jnp.asarray(np.stack(merges) if merges else np.zeros((1, SUBLANES, LANES)), F32),
        sel=jnp.asarray(sel.reshape(len(levels) * c, c), BF16),
        lv=jnp.asarray(lv),
        expand=jnp.asarray(expand, BF16),
        bc_dn=jnp.asarray(bc_dn, BF16),
        bc_beta=jnp.asarray(bc_beta, BF16),
        bc_ssd=jnp.asarray(bc_ssd, BF16),
        first_row=jnp.asarray(first_row, BF16),
        last_row=jnp.asarray(last_row, BF16),
    )
    nsquare = int(math.log2(base)) - 1
    return consts, len(levels), len(merges), nsquare


def _round_robin(*generators):
    pending = list(generators)
    while pending:
        for g in list(pending):
            try:
                next(g)
            except StopIteration:
                pending.remove(g)


def _mixer_kernel(nseq, seq_len, nlev, nmerge, nsquare, single_chunk, n_alias,
                  proj_ref, conv0_ref, dn0_ref, gla0_ref, ssm0_ref,
                  convw_ref, convb_ref, alog_ref, dtb_ref, dn_norm_ref, gla_wup_ref, gla_bup_ref, gla_norm_ref,
                  m2_d_ref, m2_norm_ref,
                  tril_ref, negi_ref, blk_ref, merge_ref, sel_ref, lv_ref, expand_ref, bc_dn_ref, bc_beta_ref,
                  bc_ssd_ref, first_row_ref, last_row_ref, *rest):
    mix_ref, convn_ref, dnn_ref, glan_ref, ssmn_ref, cbuf = rest[n_alias:]
    c = nseq * seq_len
    chunk = pl.program_id(1)
    last_chunk = pl.num_programs(1) - 1
    seq_rows = [slice(b * seq_len, (b + 1) * seq_len) for b in range(nseq)]

    def at_first_chunk(fn):
        if single_chunk:
            fn()
        else:
            pl.when(chunk == 0)(fn)

    def at_last_chunk(fn):
        if single_chunk:
            fn()
        else:
            pl.when(chunk == last_chunk)(fn)

    @at_first_chunk
    def _():
        cbuf[:, pl.ds(CONV_PAD - CONV_TAIL, CONV_TAIL), :] = conv0_ref[...]
        if not single_chunk:
            dnn_ref[...] = dn0_ref[...]
            glan_ref[...] = gla0_ref[...]
            ssmn_ref[...] = ssm0_ref[...]

    cbuf[:, pl.ds(CONV_PAD, seq_len), :] = proj_ref[:, 0:CONV_CH].reshape(nseq, seq_len, CONV_CH)

    def conv(lo, width):
        acc = convb_ref[:, lo:lo + width].reshape(1, 1, width)
        for w in range(CONV_W):
            tap = convw_ref[w:w + 1, lo:lo + width].reshape(1, 1, width)
            acc = acc + tap * cbuf[:, pl.ds(CONV_PAD - CONV_TAIL + w, seq_len), lo:lo + width]
        return _silu(acc).reshape(c, width)

    ones_rows = jnp.ones((SUBLANES_BF16, LANES), BF16)
    first_row = first_row_ref[...]

    def row_rsqrt(x, mult):
        return row_rsqrt_of_squares(x * x, mult)

    def row_rsqrt_of_squares(sq, mult):
        sums = lax.dot_general(ones_rows, sq.astype(BF16), NT_DIMS, preferred_element_type=F32)
        return _dot_exact_tn(_split3(lax.rsqrt(sums * mult + EPS)), first_row)

    def per_seq_last(x):
        parts = [jnp.broadcast_to(x[r.stop - 1:r.stop, :], (seq_len, x.shape[1])) for r in seq_rows]
        return parts[0] if nseq == 1 else jnp.concatenate(parts, axis=0)

    def cat_rows(parts):
        return parts[0] if nseq == 1 else jnp.concatenate(parts, axis=0)

    tril = tril_ref[...]
    negi = negi_ref[...]
    lv = lv_ref[...]
    diag = lv == nlev

    small = proj_ref[:, OFF_SMALL:OFF_SMALL + LANES]
    sp = _softplus(small + dtb_ref[...])
    g2 = (-LOG2E * jnp.exp(alog_ref[...])) * sp
    cum = _dot_exact_lhs(tril, _split3(g2))
    cum_t = cum.T
    last = per_seq_last(cum)
    e_cum = jnp.exp2(cum)
    e_rem = jnp.exp2(last - cum)
    e_last = jnp.exp2(last)
    cum_p, e_cum_p, e_rem_p = _split3(cum), _split3(e_cum), _split3(e_rem)

    def deltanet():
        bc_dn = bc_dn_ref[...]
        cum_col = _dot_exact_rhs(cum_p, bc_dn)
        egc_col = _dot_exact_rhs(e_cum_p, bc_dn)
        erem_col = _dot_exact_rhs(e_rem_p, bc_dn)
        beta_col = _dot_exact_rhs(_split3(jax.nn.sigmoid(small)), bc_beta_ref[...])
        dq_all = conv(OFF_DQ, DN_QK)
        dk_all = conv(OFF_DK, DN_QK)
        dv_all = conv(OFF_DV, DN_V)
        yield
        dn = []
        for h in range(DN_HEADS):
            hs = slice(h * DN_DK, (h + 1) * DN_DK)
            dq, dk = dq_all[:, hs], dk_all[:, hs]
            st = dict(q=dq * (row_rsqrt(dq, 1.0) * (DN_DK ** -0.5)), k=dk * row_rsqrt(dk, 1.0),
                      v=dv_all[:, hs], beta=beta_col[:, hs], egc=egc_col[:, hs], erem=erem_col[:, hs])
            dn.append(st)
        yield
        for h, st in enumerate(dn):
            kb, qb = st["k"].astype(BF16), st["q"].astype(BF16)
            decay = jnp.exp2(cum_col[:, h * LANES:h * LANES + c] - cum_t[SM_DA + h:SM_DA + h + 1, :] + negi)
            kk = lax.dot_general(kb, kb, NT_DIMS, preferred_element_type=F32)
            qk = lax.dot_general(qb, kb, NT_DIMS, preferred_element_type=F32)
            st["a"] = jnp.where(diag, 0.0, kk * decay * st["beta"][:, :c])
            st["qkd"] = qk * decay
        yield
        blk = blk_ref[...]
        for st in dn:
            st["p"] = st["a"] * blk
            st["n"] = -st["p"]
        for _ in range(nsquare):
            for st in dn:
                st["p"] = _dot(st["p"], st["p"])
            yield
            for st in dn:
                st["n"] = st["n"] + st["p"] + _dot(st["n"], st["p"])
            yield
        for lvl in range(nmerge):
            emask = merge_ref[lvl]
            for st in dn:
                e = st["a"] * emask
                st["y"] = e + _dot(e, st["n"])
            yield
            for st in dn:
                st["n"] = st["n"] - st["y"] - _dot(st["n"], st["y"])
            yield
        for st in dn:
            rhs = jnp.concatenate([st["k"] * (st["beta"] * st["egc"]), st["v"] * st["beta"]], axis=1)
            st["sol"] = rhs + _dot(st["n"], rhs)
        yield
        for h, st in enumerate(dn):
            sol = st["sol"]
            w_mat, u_mat = sol[:, :DN_DK], sol[:, DN_DK:]
            qe = st["q"] * st["egc"]
            v_parts, o_parts, states = [], [], []
            for b, r in enumerate(seq_rows):
                s = dn0_ref[b, h] if single_chunk else dnn_ref[b, h]
                both = _dot(jnp.concatenate([w_mat[r], qe[r]], axis=0), s)
                v_parts.append(u_mat[r] - both[:seq_len])
                o_parts.append(both[seq_len:])
                states.append(s)
            st["v_new"], st["o"], st["s"] = cat_rows(v_parts), cat_rows(o_parts), states
        yield
        for h, st in enumerate(dn):
            v_new = st["v_new"]
            o = st["o"] + _dot(st["qkd"], v_new)
            kdec = st["k"] * st["erem"]
            lane = SM_DA + h
            for b, r in enumerate(seq_rows):
                dnn_ref[b, h] = (e_last[r.start:r.start + 1, lane:lane + 1] * st["s"][b]
                                 + _dot_tn(kdec[r], v_new[r]))
            st["o"] = o
        yield
        for h, st in enumerate(dn):
            o = st["o"]
            z = proj_ref[:, OFF_DZ + h * DN_DV:OFF_DZ + (h + 1) * DN_DV]
            o_n = o * row_rsqrt(o, 1.0 / DN_DV) * dn_norm_ref[...]
            mix_ref[:, h * DN_DV:(h + 1) * DN_DV] = (o_n * _silu(z)).astype(BF16)

    def gla():
        glow = small[:, SM_GLOW:SM_GLOW + GLA_RANK]
        zg = _dot(glow, gla_wup_ref[...]) + gla_bup_ref[...]
        gk2 = _softplus(-zg) * (-LOG2E / GLA_NORMALIZER)
        gcum = _dot_exact_lhs(tril, _split3(gk2))
        yield
        gq = proj_ref[:, OFF_GQ:OFF_GQ + GLA_QK] * (GLA_DK ** -0.5)
        gkk = proj_ref[:, OFF_GK:OFF_GK + GLA_QK]
        gq_b, gkk_b = gq.astype(BF16), gkk.astype(BF16)
        gcum_p = _split3(gcum)
        gmid_all = _dot_exact_lhs(sel_ref[...], gcum_p)
        glast_cols = _dot_exact_tn(gcum_p, last_row_ref[...])
        yield
        level_ops = []
        for lev in range(nlev):
            e = jnp.exp2(-jnp.abs(gcum - gmid_all[lev * c:(lev + 1) * c, :]))
            level_ops.append(((gq * e).astype(BF16), (gkk * e).astype(BF16), lv == lev))
            if lev % 2 == 1:
                yield
        glast = per_seq_last(gcum)
        q_inter = gq * jnp.exp2(gcum)
        k_dec = gkk * jnp.exp2(glast - gcum)
        yield
        for h in range(GLA_HEADS):
            hs = slice(h * GLA_DK, (h + 1) * GLA_DK)
            v = proj_ref[:, OFF_GV + h * GLA_DV:OFF_GV + (h + 1) * GLA_DV]
            scores = jnp.where(diag, lax.dot_general(gq_b[:, hs], gkk_b[:, hs], NT_DIMS,
                                                     preferred_element_type=F32), 0.0)
            for q_t, k_t, mask in level_ops:
                s_lev = lax.dot_general(q_t[:, hs], k_t[:, hs], NT_DIMS, preferred_element_type=F32)
                scores = jnp.where(mask, s_lev, scores)
            yield
            o_parts = []
            for b, r in enumerate(seq_rows):
                s = gla0_ref[b, h] if single_chunk else glan_ref[b, h]
                o_parts.append(_dot(q_inter[r, hs], s))
                keep = jnp.exp2(glast_cols[hs, b * LANES:(b + 1) * LANES])
                glan_ref[b, h] = keep * s + _dot_tn(k_dec[r, hs], v[r])
            o = cat_rows(o_parts) + _dot(scores, v)
            yield
            z = proj_ref[:, OFF_GG + h * GLA_DV:OFF_GG + (h + 1) * GLA_DV]
            o_n = o * row_rsqrt(o, 1.0 / GLA_DV) * gla_norm_ref[...]
            mix_ref[:, DN_V + h * GLA_DV:DN_V + (h + 1) * GLA_DV] = (o_n * _silu(z)).astype(BF16)

    def ssd():
        expand = expand_ref[...]
        sp_p = _split3(sp)
        dt_x = _dot_exact_rhs(sp_p, expand)
        ecum_x = _dot_exact_rhs(e_cum_p, expand)
        dtrem_x = _dot_exact_rhs(_split3(sp * e_rem), expand)
        yield
        cum_cols = _dot_exact_rhs(cum_p, bc_ssd_ref[...])
        lane_lo = lax.broadcasted_iota(jnp.int32, (c, LANES), 1) < M2_HEADDIM
        row_lo = lax.broadcasted_iota(jnp.int32, (LANES, LANES), 0) < M2_HEADDIM
        yield
        m2_off = DN_V + GLA_V
        total = jnp.zeros((c, LANES), F32)
        gated = []
        for g in range(M2_GROUPS):
            bm = conv(OFF_MB + g * M2_STATE, M2_STATE)
            cm = conv(OFF_MC + g * M2_STATE, M2_STATE)
            bm_b, cm_b = bm.astype(BF16), cm.astype(BF16)
            cb = lax.dot_general(cm_b, bm_b, NT_DIMS, preferred_element_type=F32)
            yield
            for pp in range(M2_PAIRS_PER_GROUP):
                p = g * M2_PAIRS_PER_GROUP + pp
                ps = slice(p * LANES, (p + 1) * LANES)
                ha, hb = SM_MDT + 2 * p, SM_MDT + 2 * p + 1
                xh = conv(OFF_MX + p * LANES, LANES)
                x_dt = xh * dt_x[:, ps]
                col_a, col_b = 2 * p * LANES, (2 * p + 1) * LANES
                dec_a = jnp.exp2(cum_cols[:, col_a:col_a + c] - cum_t[ha:ha + 1, :] + negi)
                dec_b = jnp.exp2(cum_cols[:, col_b:col_b + c] - cum_t[hb:hb + 1, :] + negi)
                m_pair = jnp.concatenate([cb * dec_a, cb * dec_b], axis=1)
                x_bd = jnp.concatenate([jnp.where(lane_lo, x_dt, 0.0), jnp.where(lane_lo, 0.0, x_dt)], axis=0)
                x_rem = (xh * dtrem_x[:, ps]).astype(BF16)
                inter = []
                for b, r in enumerate(seq_rows):
                    s = ssm0_ref[b, p] if single_chunk else ssmn_ref[b, p]
                    inter.append(lax.dot_general(cm_b[r], s.astype(BF16), NT_DIMS, preferred_element_type=F32))
                    keep = jnp.where(row_lo, e_last[r.start:r.start + 1, ha:ha + 1],
                                     e_last[r.start:r.start + 1, hb:hb + 1])
                    ssmn_ref[b, p] = keep * s + lax.dot_general(x_rem[r], bm_b[r], TN_DIMS,
                                                                preferred_element_type=F32)
                y = _dot(m_pair, x_bd) + cat_rows(inter) * ecum_x[:, ps] + m2_d_ref[:, ps] * xh
                z = proj_ref[:, OFF_MZ + p * LANES:OFF_MZ + (p + 1) * LANES]
                yz = y * _silu(z)
                total = total + yz * yz
                gated.append(yz)
                yield
        scale = row_rsqrt_of_squares(total, 1.0 / M2_DIM)
        for p, yz in enumerate(gated):
            ps = slice(p * LANES, (p + 1) * LANES)
            mix_ref[:, m2_off + p * LANES:m2_off + (p + 1) * LANES] = (yz * scale * m2_norm_ref[:, ps]).astype(BF16)

    _round_robin(deltanet(), gla(), ssd())

    @at_last_chunk
    def _():
        convn_ref[...] = cbuf[:, pl.ds(CONV_PAD + seq_len - CONV_TAIL, CONV_TAIL), :]

    if not single_chunk:
        cbuf[:, pl.ds(0, CONV_PAD), :] = cbuf[:, pl.ds(seq_len, CONV_PAD), :]


def _mixer(proj, row_start, nb, seq, nseq, seq_len, layer, conv0, dn0, gla0, ssm0, params,
           total_rows, prev_mix=None, prev_states=None):
    c = nseq * seq_len
    nc = seq // seq_len
    blk0 = row_start // c
    lsel = layer if conv0.shape[0] > 1 else 0
    consts, nlev, nmerge, nsquare = _mixer_constants(nseq, seq_len)
    const_args = [consts[k] for k in CONST_NAMES]

    def state_spec(shape):
        zeros = (0,) * len(shape)
        return pl.BlockSpec((None, nseq) + shape, lambda b, k: (lsel, b) + zeros)

    def out_state_spec(shape):
        zeros = (0,) * len(shape)
        return pl.BlockSpec((None, nseq) + shape, lambda b, k: (layer, b) + zeros)

    conv_shape = (CONV_TAIL, CONV_CH)
    dn_shape = (DN_HEADS, DN_DK, DN_DV)
    gla_shape = (GLA_HEADS, GLA_DK, GLA_DV)
    ssm_shape = (M2_PAIRS, 2 * M2_HEADDIM, M2_STATE)
    state_shapes = (conv_shape, dn_shape, gla_shape, ssm_shape)
    in_specs = [pl.BlockSpec((c, PROJ_COLS), lambda b, k: (blk0 + b * nc + k, 0))]
    in_specs += [state_spec(s) for s in state_shapes]
    in_specs += [_resident(p.shape) for p in params]
    in_specs += [_resident(a.shape) for a in const_args]
    args = [proj, conv0, dn0, gla0, ssm0, *params, *const_args]
    aliases = {}
    if prev_mix is not None:
        aliases[len(args)] = 0
        args.append(prev_mix)
    if prev_states is not None:
        for i, st in enumerate(prev_states):
            aliases[len(args)] = 1 + i
            args.append(st)
    n_alias = len(aliases)
    in_specs += [pl.BlockSpec(memory_space=pl.ANY)] * n_alias
    out_specs = [pl.BlockSpec((c, D_MIX), lambda b, k: (blk0 + b * nc + k, 0))]
    out_specs += [out_state_spec(s) for s in state_shapes]
    out_shape = [jax.ShapeDtypeStruct((total_rows, D_MIX), BF16)]
    out_shape += [jax.ShapeDtypeStruct((DEPTH, nb) + s, F32) for s in state_shapes]
    body = functools.partial(_mixer_kernel, nseq, seq_len, nlev, nmerge, nsquare, nc == 1, n_alias)
    return pl.pallas_call(
        body, grid=(nb // nseq, nc), in_specs=in_specs, out_specs=out_specs, out_shape=out_shape,
        input_output_aliases=aliases,
        scratch_shapes=[pltpu.VMEM((nseq, CONV_PAD + seq_len, CONV_CH), F32)],
        compiler_params=pltpu.CompilerParams(dimension_semantics=("parallel", "arbitrary"),
                                             vmem_limit_bytes=VMEM_LIMIT_BYTES),
        name=f"mixer_{nseq}x{seq_len}")(*args)


def _permute_w_in(w):
    o = CONV_CH
    dz = w[:, o:o + DN_V]; o += DN_V
    da = w[:, o:o + DN_HEADS]; o += DN_HEADS
    db = w[:, o:o + DN_HEADS]; o += DN_HEADS
    gq = w[:, o:o + GLA_QK]; o += GLA_QK
    gk = w[:, o:o + GLA_QK]; o += GLA_QK
    gv = w[:, o:o + GLA_V]; o += GLA_V
    glow = w[:, o:o + GLA_RANK]; o += GLA_RANK
    gg = w[:, o:o + GLA_V]; o += GLA_V
    mz = w[:, o:o + M2_DIM]; o += M2_DIM
    mdt = w[:, o:o + M2_HEADS]; o += M2_HEADS
    assert o == w.shape[1]
    pad = jnp.zeros((w.shape[0], LANES - SM_USED), w.dtype)
    return jnp.concatenate([w[:, :CONV_CH], dz, gq, gk, gv, gg, mz, da, mdt, db, glow, pad], axis=1)


def _small_lane_row(dn_vals, m2_vals):
    pad = jnp.zeros((LANES - SM_DB,), F32)
    return jnp.concatenate([dn_vals.astype(F32), m2_vals.astype(F32), pad]).reshape(1, LANES)


def _ssm_to_pairs(s):
    t = jnp.swapaxes(s, -1, -2)
    return t.reshape(t.shape[:-3] + (M2_PAIRS, 2 * M2_HEADDIM, M2_STATE))


def _ssm_from_pairs(s):
    t = s.reshape(s.shape[:-3] + (M2_HEADS, M2_HEADDIM, M2_STATE))
    return jnp.swapaxes(t, -1, -2)


def kernel(x_prompt, x_sample, state_conv, state_delta, state_gla, state_ssm, ffn1_norm, ffn1_w_gu, ffn1_w_down, mix_norm, w_in, conv_w, conv_b, dn_a_log, dn_dt_bias, dn_norm, gla_w_up, gla_b_up, gla_norm, m2_a_log, m2_dt_bias, m2_d, m2_norm, w_out, ffn2_norm, ffn2_w_gu, ffn2_w_down, final_norm):
    bp, sp, _ = x_prompt.shape
    bs, ss, _ = x_sample.shape
    tp, ts = bp * sp, bs * ss
    assert tp % DENSE_TILE == 0 and ts % DENSE_TILE == 0 and sp % PROMPT_CHUNK == 0
    assert bs % SAMPLE_SEQS_PER_STEP == 0 and tp % (SAMPLE_SEQS_PER_STEP * ss) == 0 and ss % SUBLANES == 0

    zero_conv = jnp.zeros((1, bp, CONV_TAIL, CONV_CH), F32)
    zero_dn = jnp.zeros((1, bp, DN_HEADS, DN_DK, DN_DV), F32)
    zero_gla = jnp.zeros((1, bp, GLA_HEADS, GLA_DK, GLA_DV), F32)
    zero_ssm = jnp.zeros((1, bp, M2_PAIRS, 2 * M2_HEADDIM, M2_STATE), F32)
    ssm_pairs = _ssm_to_pairs(state_ssm)

    xs = [x_prompt.reshape(tp, D_MODEL), x_sample.reshape(ts, D_MODEL)]
    st_p = st_s = None
    for l in range(DEPTH):
        wg1, wu1 = ffn1_w_gu[l, :, :D_FF].astype(BF16), ffn1_w_gu[l, :, D_FF:].astype(BF16)
        wg2, wu2 = ffn2_w_gu[l, :, :D_FF].astype(BF16), ffn2_w_gu[l, :, D_FF:].astype(BF16)
        params = (conv_w[l], conv_b[l].reshape(1, CONV_CH),
                  _small_lane_row(dn_a_log[l], m2_a_log[l]), _small_lane_row(dn_dt_bias[l], m2_dt_bias[l]),
                  dn_norm[l].reshape(1, DN_DV),
                  gla_w_up[l].astype(BF16), gla_b_up[l].reshape(1, GLA_QK), gla_norm[l].reshape(1, GLA_DV),
                  jnp.repeat(m2_d[l], M2_HEADDIM).reshape(1, M2_DIM), m2_norm[l].reshape(1, M2_DIM))

        x = _ffn(xs, ffn1_norm[l], wg1, wu1, ffn1_w_down[l].astype(BF16))
        proj = _inproj(x, mix_norm[l], _permute_w_in(w_in[l]).astype(BF16))
        mix, *st_p = _mixer(proj, 0, bp, sp, 1, PROMPT_CHUNK, l, zero_conv, zero_dn, zero_gla, zero_ssm, params,
                            tp + ts, prev_states=st_p)
        mix, *st_s = _mixer(proj, tp, bs, ss, SAMPLE_SEQS_PER_STEP, ss, l,
                            state_conv, state_delta, state_gla, ssm_pairs, params,
                            tp + ts, prev_mix=mix, prev_states=st_s)
        x = _outproj(x, mix, w_out[l].astype(BF16))
        if l < DEPTH - 1:
            xs = [_ffn([x], ffn2_norm[l], wg2, wu2, ffn2_w_down[l].astype(BF16))]
        else:
            y_prompt, y_sample = _ffn([x], ffn2_norm[l], wg2, wu2, ffn2_w_down[l].astype(BF16),
                                      final_gain=final_norm, split_rows=tp)

    return (y_prompt.reshape(bp, sp, D_MODEL), y_sample.reshape(bs, ss, D_MODEL),
            st_p[0], st_p[1], st_p[2], _ssm_from_pairs(st_p[3]),
            st_s[0], st_s[1], st_s[2], _ssm_from_pairs(st_s[3]))
```

```python
import functools
import math

import jax
import jax.numpy as jnp
import numpy as np
from jax import lax
from jax.experimental import pallas as pl
from jax.experimental.pallas import tpu as pltpu

F32 = jnp.float32
BF16 = jnp.bfloat16
LOG2E = math.log2(math.e)

D_MODEL = 1024
DEPTH = 2
D_MIX = 2 * D_MODEL
D_FF = 2816
CONV_W = 4
DN_HEADS = 4
DN_DK = 128
DN_DV = 128
GLA_HEADS = 4
GLA_DK = 64
GLA_DV = 128
GLA_RANK = 16
GLA_NORMALIZER = 16.0
M2_DIM = 1024
M2_HEADDIM = 64
M2_HEADS = 16
M2_STATE = 128
M2_GROUPS = 2
M2_PAIRS = M2_HEADS // 2
M2_PAIRS_PER_GROUP = M2_PAIRS // M2_GROUPS
EPS = 1e-6
LANES = 128
SUBLANES = 8

DN_QK = DN_HEADS * DN_DK
DN_V = DN_HEADS * DN_DV
GLA_QK = GLA_HEADS * GLA_DK
GLA_V = GLA_HEADS * GLA_DV
M2_BC = M2_GROUPS * M2_STATE
CONV_CH = 3 * 512 + M2_DIM + 2 * M2_BC

OFF_DQ = 0
OFF_DK = OFF_DQ + DN_QK
OFF_DV = OFF_DK + DN_QK
OFF_MX = OFF_DV + DN_V
OFF_MB = OFF_MX + M2_DIM
OFF_MC = OFF_MB + M2_BC
OFF_DZ = CONV_CH
OFF_GQ = OFF_DZ + DN_V
OFF_GK = OFF_GQ + GLA_QK
OFF_GV = OFF_GK + GLA_QK
OFF_GG = OFF_GV + GLA_V
OFF_MZ = OFF_GG + GLA_V
OFF_SMALL = OFF_MZ + M2_DIM
SM_DA = 0
SM_MDT = SM_DA + DN_HEADS
SM_DB = SM_MDT + M2_HEADS
SM_GLOW = SM_DB + DN_HEADS
SM_USED = SM_GLOW + GLA_RANK
PROJ_COLS = OFF_SMALL + LANES

CONV_PAD = SUBLANES
CONV_TAIL = CONV_W - 1
NEUMANN_BLOCK = 16
VMEM_LIMIT_BYTES = 56 * 1024 * 1024
PROMPT_CHUNK = 128
SAMPLE_SEQS_PER_STEP = 8
DENSE_TILE = 256
MASKED = -1e30
NT_DIMS = (((1,), (1,)), ((), ()))
TN_DIMS = (((0,), (0,)), ((), ()))


def _dot(a, b):
    return jnp.dot(a.astype(BF16), b.astype(BF16), preferred_element_type=F32)


def _dot_nt(a, b):
    return lax.dot_general(a.astype(BF16), b.astype(BF16), NT_DIMS, preferred_element_type=F32)


def _dot_tn(a, b):
    return lax.dot_general(a.astype(BF16), b.astype(BF16), TN_DIMS, preferred_element_type=F32)


def _split3(x):
    hi = x.astype(BF16)
    r1 = x - hi.astype(F32)
    mid = r1.astype(BF16)
    lo = (r1 - mid.astype(F32)).astype(BF16)
    return hi, mid, lo


def _sum_dots(pieces, fn):
    acc = None
    for piece in pieces:
        part = fn(piece)
        acc = part if acc is None else acc + part
    return acc


def _dot_exact_lhs(m, pieces):
    return _sum_dots(pieces, lambda p: jnp.dot(m, p, preferred_element_type=F32))


def _dot_exact_rhs(pieces, m):
    return _sum_dots(pieces, lambda p: jnp.dot(p, m, preferred_element_type=F32))


def _dot_exact_tn(pieces, m):
    return _sum_dots(pieces, lambda p: lax.dot_general(p, m, TN_DIMS, preferred_element_type=F32))


def _silu(x):
    return x * jax.nn.sigmoid(x)


def _softplus(x):
    return jnp.maximum(x, 0.0) + jnp.log(1.0 + jnp.exp(-jnp.abs(x)))


def _rms(x, gain):
    return x * lax.rsqrt(jnp.mean(x * x, axis=-1, keepdims=True) + EPS) * gain


def _ffn_kernel(split_tile, final_norm, *refs):
    refs = list(refs)
    two_in = split_tile is not None and not final_norm
    two_out = split_tile is not None and final_norm
    x_refs = [refs.pop(0) for _ in range(2 if two_in else 1)]
    g_ref, wg_ref, wu_ref, wd_ref = (refs.pop(0) for _ in range(4))
    fg_ref = refs.pop(0) if final_norm else None
    o_refs = refs
    i = pl.program_id(0)
    x = jnp.where(i < split_tile, x_refs[0][...], x_refs[1][...]) if two_in else x_refs[0][...]
    xn = _rms(x, g_ref[...]).astype(BF16)
    gate = jnp.dot(xn, wg_ref[...], preferred_element_type=F32)
    up = jnp.dot(xn, wu_ref[...], preferred_element_type=F32)
    h = (_silu(gate) * up).astype(BF16)
    y = x + 0.5 * jnp.dot(h, wd_ref[...], preferred_element_type=F32)
    if final_norm:
        y = _rms(y, fg_ref[...])
    if two_out:
        @pl.when(i < split_tile)
        def _():
            o_refs[0][...] = y

        @pl.when(i >= split_tile)
        def _():
            o_refs[1][...] = y
    else:
        o_refs[0][...] = y


def _inproj_kernel(x_ref, g_ref, w_ref, o_ref):
    xn = _rms(x_ref[...], g_ref[...]).astype(BF16)
    o_ref[...] = jnp.dot(xn, w_ref[...], preferred_element_type=F32)


def _outproj_kernel(x_ref, m_ref, w_ref, o_ref):
    o_ref[...] = x_ref[...] + jnp.dot(m_ref[...], w_ref[...], preferred_element_type=F32)


def _resident(shape):
    nd = len(shape)
    return pl.BlockSpec(shape, lambda *_: (0,) * nd, pipeline_mode=pl.Buffered(1))


def _dense_params():
    return pltpu.CompilerParams(dimension_semantics=("parallel",), vmem_limit_bytes=VMEM_LIMIT_BYTES)


def _ffn(xs, gain, wg, wu, wd, final_gain=None, split_rows=None):
    t = sum(x.shape[0] for x in xs)
    two_in = len(xs) == 2
    two_out = final_gain is not None and split_rows is not None
    split_tile = None
    if two_in or two_out:
        split_tile = (xs[0].shape[0] if two_in else split_rows) // DENSE_TILE
    tile = pl.BlockSpec((DENSE_TILE, D_MODEL), lambda i: (i, 0))
    first = pl.BlockSpec((DENSE_TILE, D_MODEL), lambda i: (jnp.minimum(i, split_tile - 1), 0))
    second = pl.BlockSpec((DENSE_TILE, D_MODEL), lambda i: (jnp.maximum(i - split_tile, 0), 0))
    in_specs = ([first, second] if two_in else [tile]) + [
        _resident((1, D_MODEL)), _resident(wg.shape), _resident(wu.shape), _resident(wd.shape)]
    args = list(xs) + [gain.reshape(1, D_MODEL), wg, wu, wd]
    if final_gain is not None:
        in_specs.append(_resident((1, D_MODEL)))
        args.append(final_gain.reshape(1, D_MODEL))
    if two_out:
        out_specs = [first, second]
        out_shape = [jax.ShapeDtypeStruct((split_rows, D_MODEL), F32),
                     jax.ShapeDtypeStruct((t - split_rows, D_MODEL), F32)]
    else:
        out_specs, out_shape = tile, jax.ShapeDtypeStruct((t, D_MODEL), F32)
    return pl.pallas_call(
        functools.partial(_ffn_kernel, split_tile, final_gain is not None),
        grid=(t // DENSE_TILE,), in_specs=in_specs, out_specs=out_specs, out_shape=out_shape,
        compiler_params=pltpu.CompilerParams(dimension_semantics=("arbitrary",),
                                             vmem_limit_bytes=VMEM_LIMIT_BYTES),
        name="ffn" if final_gain is None else "ffn_final_norm")(*args)


def _inproj(x, gain, w):
    t = x.shape[0]
    return pl.pallas_call(
        _inproj_kernel, grid=(t // DENSE_TILE,),
        in_specs=[pl.BlockSpec((DENSE_TILE, D_MODEL), lambda i: (i, 0)), _resident((1, D_MODEL)), _resident(w.shape)],
        out_specs=pl.BlockSpec((DENSE_TILE, PROJ_COLS), lambda i: (i, 0)),
        out_shape=jax.ShapeDtypeStruct((t, PROJ_COLS), F32), compiler_params=_dense_params(),
        name="inproj")(x, gain.reshape(1, D_MODEL), w)


def _outproj(x, mix, w):
    t = x.shape[0]
    return pl.pallas_call(
        _outproj_kernel, grid=(t // DENSE_TILE,),
        in_specs=[pl.BlockSpec((DENSE_TILE, D_MODEL), lambda i: (i, 0)),
                  pl.BlockSpec((DENSE_TILE, D_MIX), lambda i: (i, 0)), _resident(w.shape)],
        out_specs=pl.BlockSpec((DENSE_TILE, D_MODEL), lambda i: (i, 0)),
        out_shape=jax.ShapeDtypeStruct((t, D_MODEL), F32), compiler_params=_dense_params(),
        name="outproj")(x, mix, w)


CONST_NAMES = ("tril", "negi", "blk", "merge", "sel", "lv", "expand", "bc_dn", "bc_beta", "bc_ssd")


def _mixer_constants(nseq, seq_len):
    c = nseq * seq_len
    idx = np.arange(c)
    seq, pos = idx // seq_len, idx % seq_len
    same = seq[:, None] == seq[None, :]
    pi, pj = pos[:, None], pos[None, :]
    causal = same & (pi >= pj)
    base = min(seq_len, NEUMANN_BLOCK)
    merges = []
    s = base
    while s < seq_len:
        merges.append(same & (pi // (2 * s) == pj // (2 * s)) & (pi // s != pj // s))
        s *= 2
    levels = []
    m = seq_len // 2
    while m >= 1:
        levels.append(m)
        m //= 2
    sel = np.zeros((len(levels), c, c), np.float32)
    lv = np.full((c, c), -1, np.int32)
    xor = pi ^ pj
    for k, m in enumerate(levels):
        mid = seq * seq_len + (pos // (2 * m)) * (2 * m) + (m - 1)
        sel[k, idx, mid] = 1.0
        lv[same & (pi > pj) & (xor >= m) & (xor < 2 * m)] = k
    lv[idx, idx] = len(levels)
    expand = np.zeros((LANES, M2_DIM), np.float32)
    bc_ssd = np.zeros((LANES, M2_HEADS * LANES), np.float32)
    for h in range(M2_HEADS):
        expand[SM_MDT + h, h * M2_HEADDIM:(h + 1) * M2_HEADDIM] = 1.0
        bc_ssd[SM_MDT + h, h * LANES:(h + 1) * LANES] = 1.0
    bc_dn = np.zeros((LANES, DN_HEADS * LANES), np.float32)
    bc_beta = np.zeros((LANES, DN_HEADS * LANES), np.float32)
    for h in range(DN_HEADS):
        bc_dn[SM_DA + h, h * LANES:(h + 1) * LANES] = 1.0
        bc_beta[SM_DB + h, h * LANES:(h + 1) * LANES] = 1.0
    consts = dict(
        tril=jnp.asarray(causal, BF16),
        negi=jnp.asarray(np.where(causal, 0.0, MASKED), F32),
        blk=jnp.asarray(same & (pi // base == pj // base), F32),
        merge=jnp.asarray(np.stack(merges) if merges else np.zeros((1, SUBLANES, LANES)), F32),
        sel=jnp.asarray(sel.reshape(len(levels) * c, c), BF16),
        lv=jnp.asarray(lv),
        expand=jnp.asarray(expand, BF16),
        bc_dn=jnp.asarray(bc_dn, BF16),
        bc_beta=jnp.asarray(bc_beta, BF16),
        bc_ssd=jnp.asarray(bc_ssd, BF16),
    )
    nsquare = int(math.log2(base)) - 1
    return consts, len(levels), len(merges), nsquare


def _round_robin(*generators):
    pending = list(generators)
    while pending:
        for g in list(pending):
            try:
                next(g)
            except StopIteration:
                pending.remove(g)


def _mixer_kernel(nseq, seq_len, nlev, nmerge, nsquare, single_chunk, n_alias,
                  proj_ref, conv0_ref, dn0_ref, gla0_ref, ssm0_ref,
                  convw_ref, convb_ref, alog_ref, dtb_ref, dn_norm_ref, gla_wup_ref, gla_bup_ref, gla_norm_ref,
                  m2_d_ref, m2_norm_ref,
                  tril_ref, negi_ref, blk_ref, merge_ref, sel_ref, lv_ref, expand_ref, bc_dn_ref, bc_beta_ref,
                  bc_ssd_ref, *rest):
    mix_ref, convn_ref, dnn_ref, glan_ref, ssmn_ref, cbuf = rest[n_alias:]
    c = nseq * seq_len
    chunk = pl.program_id(1)
    last_chunk = pl.num_programs(1) - 1
    seq_rows = [slice(b * seq_len, (b + 1) * seq_len) for b in range(nseq)]

    def at_first_chunk(fn):
        if single_chunk:
            fn()
        else:
            pl.when(chunk == 0)(fn)

    def at_last_chunk(fn):
        if single_chunk:
            fn()
        else:
            pl.when(chunk == last_chunk)(fn)

    @at_first_chunk
    def _():
        cbuf[:, pl.ds(CONV_PAD - CONV_TAIL, CONV_TAIL), :] = conv0_ref[...]
        if not single_chunk:
            dnn_ref[...] = dn0_ref[...]
            glan_ref[...] = gla0_ref[...]
            ssmn_ref[...] = ssm0_ref[...]

    cbuf[:, pl.ds(CONV_PAD, seq_len), :] = proj_ref[:, 0:CONV_CH].reshape(nseq, seq_len, CONV_CH)

    def conv(lo, width):
        acc = convb_ref[:, lo:lo + width].reshape(1, 1, width)
        for w in range(CONV_W):
            tap = convw_ref[w:w + 1, lo:lo + width].reshape(1, 1, width)
            acc = acc + tap * cbuf[:, pl.ds(CONV_PAD - CONV_TAIL + w, seq_len), lo:lo + width]
        return _silu(acc).reshape(c, width)

    ones_sq = jnp.ones((LANES, LANES), BF16)

    def row_rsqrt(x, mult):
        return row_rsqrt_of_squares(x * x, mult)

    def row_rsqrt_of_squares(sq, mult):
        sums = jnp.dot(sq.astype(BF16), ones_sq, preferred_element_type=F32)
        return lax.rsqrt(sums * mult + EPS)

    def per_seq_last(x):
        parts = [jnp.broadcast_to(x[r.stop - 1:r.stop, :], (seq_len, x.shape[1])) for r in seq_rows]
        return parts[0] if nseq == 1 else jnp.concatenate(parts, axis=0)

    def cat_rows(parts):
        return parts[0] if nseq == 1 else jnp.concatenate(parts, axis=0)

    tril = tril_ref[...]
    negi = negi_ref[...]
    lv = lv_ref[...]
    diag = lv == nlev

    small = proj_ref[:, OFF_SMALL:OFF_SMALL + LANES]
    sp = _softplus(small + dtb_ref[...])
    g2 = (-LOG2E * jnp.exp(alog_ref[...])) * sp
    cum = _dot_exact_lhs(tril, _split3(g2))
    cum_t = cum.T
    last = per_seq_last(cum)
    e_cum = jnp.exp2(cum)
    e_rem = jnp.exp2(last - cum)
    e_last = jnp.exp2(last)
    cum_p, e_cum_p, e_rem_p = _split3(cum), _split3(e_cum), _split3(e_rem)

    def deltanet():
        bc_dn = bc_dn_ref[...]
        cum_col = _dot_exact_rhs(cum_p, bc_dn)
        egc_col = _dot_exact_rhs(e_cum_p, bc_dn)
        erem_col = _dot_exact_rhs(e_rem_p, bc_dn)
        beta_col = _dot_exact_rhs(_split3(jax.nn.sigmoid(small)), bc_beta_ref[...])
        dq_all = conv(OFF_DQ, DN_QK)
        dk_all = conv(OFF_DK, DN_QK)
        dv_all = conv(OFF_DV, DN_V)
        yield
        dn = []
        for h in range(DN_HEADS):
            hs = slice(h * DN_DK, (h + 1) * DN_DK)
            dq, dk = dq_all[:, hs], dk_all[:, hs]
            st = dict(q=dq * (row_rsqrt(dq, 1.0) * (DN_DK ** -0.5)), k=dk * row_rsqrt(dk, 1.0),
                      v=dv_all[:, hs], beta=beta_col[:, hs], egc=egc_col[:, hs], erem=erem_col[:, hs])
            dn.append(st)
        yield
        for h, st in enumerate(dn):
            kb, qb = st["k"].astype(BF16), st["q"].astype(BF16)
            decay = jnp.exp2(cum_col[:, h * LANES:h * LANES + c] - cum_t[SM_DA + h:SM_DA + h + 1, :] + negi)
            kk = lax.dot_general(kb, kb, NT_DIMS, preferred_element_type=F32)
            qk = lax.dot_general(qb, kb, NT_DIMS, preferred_element_type=F32)
            st["a"] = jnp.where(diag, 0.0, kk * decay * st["beta"][:, :c])
            st["qkd"] = qk * decay
        yield
        blk = blk_ref[...]
        for st in dn:
            st["p"] = st["a"] * blk
            st["n"] = -st["p"]
        for _ in range(nsquare):
            for st in dn:
                st["p"] = _dot(st["p"], st["p"])
            yield
            for st in dn:
                st["n"] = st["n"] + st["p"] + _dot(st["n"], st["p"])
            yield
        for lvl in range(nmerge):
            emask = merge_ref[lvl]
            for st in dn:
                e = st["a"] * emask
                st["y"] = e + _dot(e, st["n"])
            yield
            for st in dn:
                st["n"] = st["n"] - st["y"] - _dot(st["n"], st["y"])
            yield
        for st in dn:
            rhs = jnp.concatenate([st["k"] * (st["beta"] * st["egc"]), st["v"] * st["beta"]], axis=1)
            st["sol"] = rhs + _dot(st["n"], rhs)
        yield
        for h, st in enumerate(dn):
            sol = st["sol"]
            w_mat, u_mat = sol[:, :DN_DK], sol[:, DN_DK:]
            qe = st["q"] * st["egc"]
            v_parts, o_parts, states = [], [], []
            for b, r in enumerate(seq_rows):
                s = dn0_ref[b, h] if single_chunk else dnn_ref[b, h]
                both = _dot(jnp.concatenate([w_mat[r], qe[r]], axis=0), s)
                v_parts.append(u_mat[r] - both[:seq_len])
                o_parts.append(both[seq_len:])
                states.append(s)
            st["v_new"], st["o"], st["s"] = cat_rows(v_parts), cat_rows(o_parts), states
        yield
        for h, st in enumerate(dn):
            v_new = st["v_new"]
            o = st["o"] + _dot(st["qkd"], v_new)
            kdec = st["k"] * st["erem"]
            lane = SM_DA + h
            for b, r in enumerate(seq_rows):
                dnn_ref[b, h] = (e_last[r.start:r.start + 1, lane:lane + 1] * st["s"][b]
                                 + _dot_tn(kdec[r], v_new[r]))
            st["o"] = o
        yield
        for h, st in enumerate(dn):
            o = st["o"]
            z = proj_ref[:, OFF_DZ + h * DN_DV:OFF_DZ + (h + 1) * DN_DV]
            o_n = o * row_rsqrt(o, 1.0 / DN_DV) * dn_norm_ref[...]
            mix_ref[:, h * DN_DV:(h + 1) * DN_DV] = (o_n * _silu(z)).astype(BF16)

    def gla():
        glow = small[:, SM_GLOW:SM_GLOW + GLA_RANK]
        zg = _dot(glow, gla_wup_ref[...]) + gla_bup_ref[...]
        gk2 = _softplus(-zg) * (-LOG2E / GLA_NORMALIZER)
        gcum = _dot_exact_lhs(tril, _split3(gk2))
        yield
        gq = proj_ref[:, OFF_GQ:OFF_GQ + GLA_QK] * (GLA_DK ** -0.5)
        gkk = proj_ref[:, OFF_GK:OFF_GK + GLA_QK]
        gq_b, gkk_b = gq.astype(BF16), gkk.astype(BF16)
        gcum_p = _split3(gcum)
        gmid_all = _dot_exact_lhs(sel_ref[...], gcum_p)
        yield
        level_ops = []
        for lev in range(nlev):
            e = jnp.exp2(-jnp.abs(gcum - gmid_all[lev * c:(lev + 1) * c, :]))
            level_ops.append(((gq * e).astype(BF16), (gkk * e).astype(BF16), lv == lev))
            if lev % 2 == 1:
                yield
        glast = per_seq_last(gcum)
        q_inter = gq * jnp.exp2(gcum)
        k_dec = gkk * jnp.exp2(glast - gcum)
        eye_dk = (lax.broadcasted_iota(jnp.int32, (GLA_DK, GLA_DK), 0)
                  == lax.broadcasted_iota(jnp.int32, (GLA_DK, GLA_DK), 1))
        ones_dk = jnp.ones((GLA_DK, GLA_DV), BF16)
        yield
        for h in range(GLA_HEADS):
            hs = slice(h * GLA_DK, (h + 1) * GLA_DK)
            v = proj_ref[:, OFF_GV + h * GLA_DV:OFF_GV + (h + 1) * GLA_DV]
            scores = jnp.where(diag, lax.dot_general(gq_b[:, hs], gkk_b[:, hs], NT_DIMS,
                                                     preferred_element_type=F32), 0.0)
            for q_t, k_t, mask in level_ops:
                s_lev = lax.dot_general(q_t[:, hs], k_t[:, hs], NT_DIMS, preferred_element_type=F32)
                scores = jnp.where(mask, s_lev, scores)
            yield
            o_parts = []
            for b, r in enumerate(seq_rows):
                s = gla0_ref[b, h] if single_chunk else glan_ref[b, h]
                o_parts.append(_dot(q_inter[r, hs], s))
                lam = jnp.where(eye_dk, jnp.exp2(glast[r.start:r.start + 1, hs]), 0.0)
                keep = _dot_exact_rhs(_split3(lam), ones_dk)
                glan_ref[b, h] = keep * s + _dot_tn(k_dec[r, hs], v[r])
            o = cat_rows(o_parts) + _dot(scores, v)
            yield
            z = proj_ref[:, OFF_GG + h * GLA_DV:OFF_GG + (h + 1) * GLA_DV]
            o_n = o * row_rsqrt(o, 1.0 / GLA_DV) * gla_norm_ref[...]
            mix_ref[:, DN_V + h * GLA_DV:DN_V + (h + 1) * GLA_DV] = (o_n * _silu(z)).astype(BF16)

    def ssd():
        expand = expand_ref[...]
        sp_p = _split3(sp)
        dt_x = _dot_exact_rhs(sp_p, expand)
        ecum_x = _dot_exact_rhs(e_cum_p, expand)
        dtrem_x = _dot_exact_rhs(_split3(sp * e_rem), expand)
        yield
        cum_cols = _dot_exact_rhs(cum_p, bc_ssd_ref[...])
        lane_lo = lax.broadcasted_iota(jnp.int32, (c, LANES), 1) < M2_HEADDIM
        row_lo = lax.broadcasted_iota(jnp.int32, (LANES, LANES), 0) < M2_HEADDIM
        yield
        m2_off = DN_V + GLA_V
        total = jnp.zeros((c, LANES), F32)
        gated = []
        for g in range(M2_GROUPS):
            bm = conv(OFF_MB + g * M2_STATE, M2_STATE)
            cm = conv(OFF_MC + g * M2_STATE, M2_STATE)
            bm_b, cm_b = bm.astype(BF16), cm.astype(BF16)
            cb = lax.dot_general(cm_b, bm_b, NT_DIMS, preferred_element_type=F32)
            yield
            for pp in range(M2_PAIRS_PER_GROUP):
                p = g * M2_PAIRS_PER_GROUP + pp
                ps = slice(p * LANES, (p + 1) * LANES)
                ha, hb = SM_MDT + 2 * p, SM_MDT + 2 * p + 1
                xh = conv(OFF_MX + p * LANES, LANES)
                x_dt = xh * dt_x[:, ps]
                col_a, col_b = 2 * p * LANES, (2 * p + 1) * LANES
                dec_a = jnp.exp2(cum_cols[:, col_a:col_a + c] - cum_t[ha:ha + 1, :] + negi)
                dec_b = jnp.exp2(cum_cols[:, col_b:col_b + c] - cum_t[hb:hb + 1, :] + negi)
                m_pair = jnp.concatenate([cb * dec_a, cb * dec_b], axis=1)
                x_bd = jnp.concatenate([jnp.where(lane_lo, x_dt, 0.0), jnp.where(lane_lo, 0.0, x_dt)], axis=0)
                x_rem = (xh * dtrem_x[:, ps]).astype(BF16)
                inter = []
                for b, r in enumerate(seq_rows):
                    s = ssm0_ref[b, p] if single_chunk else ssmn_ref[b, p]
                    inter.append(lax.dot_general(cm_b[r], s.astype(BF16), NT_DIMS, preferred_element_type=F32))
                    keep = jnp.where(row_lo, e_last[r.start:r.start + 1, ha:ha + 1],
                                     e_last[r.start:r.start + 1, hb:hb + 1])
                    ssmn_ref[b, p] = keep * s + lax.dot_general(x_rem[r], bm_b[r], TN_DIMS,
                                                                preferred_element_type=F32)
                y = _dot(m_pair, x_bd) + cat_rows(inter) * ecum_x[:, ps] + m2_d_ref[:, ps] * xh
                z = proj_ref[:, OFF_MZ + p * LANES:OFF_MZ + (p + 1) * LANES]
                yz = y * _silu(z)
                total = total + yz * yz
                gated.append(yz)
                yield
        scale = row_rsqrt_of_squares(total, 1.0 / M2_DIM)
        for p, yz in enumerate(gated):
            ps = slice(p * LANES, (p + 1) * LANES)
            mix_ref[:, m2_off + p * LANES:m2_off + (p + 1) * LANES] = (yz * scale * m2_norm_ref[:, ps]).astype(BF16)

    _round_robin(deltanet(), gla(), ssd())

    @at_last_chunk
    def _():
        convn_ref[...] = cbuf[:, pl.ds(CONV_PAD + seq_len - CONV_TAIL, CONV_TAIL), :]

    if not single_chunk:
        cbuf[:, pl.ds(0, CONV_PAD), :] = cbuf[:, pl.ds(seq_len, CONV_PAD), :]


def _mixer(proj, row_start, nb, seq, nseq, seq_len, layer, conv0, dn0, gla0, ssm0, params,
           total_rows, prev_mix=None, prev_states=None):
    c = nseq * seq_len
    nc = seq // seq_len
    blk0 = row_start // c
    lsel = layer if conv0.shape[0] > 1 else 0
    consts, nlev, nmerge, nsquare = _mixer_constants(nseq, seq_len)
    const_args = [consts[k] for k in CONST_NAMES]

    def state_spec(shape):
        zeros = (0,) * len(shape)
        return pl.BlockSpec((None, nseq) + shape, lambda b, k: (lsel, b) + zeros)

    def out_state_spec(shape):
        zeros = (0,) * len(shape)
        return pl.BlockSpec((None, nseq) + shape, lambda b, k: (layer, b) + zeros)

    conv_shape = (CONV_TAIL, CONV_CH)
    dn_shape = (DN_HEADS, DN_DK, DN_DV)
    gla_shape = (GLA_HEADS, GLA_DK, GLA_DV)
    ssm_shape = (M2_PAIRS, 2 * M2_HEADDIM, M2_STATE)
    state_shapes = (conv_shape, dn_shape, gla_shape, ssm_shape)
    in_specs = [pl.BlockSpec((c, PROJ_COLS), lambda b, k: (blk0 + b * nc + k, 0))]
    in_specs += [state_spec(s) for s in state_shapes]
    in_specs += [_resident(p.shape) for p in params]
    in_specs += [_resident(a.shape) for a in const_args]
    args = [proj, conv0, dn0, gla0, ssm0, *params, *const_args]
    aliases = {}
    if prev_mix is not None:
        aliases[len(args)] = 0
        args.append(prev_mix)
    if prev_states is not None:
        for i, st in enumerate(prev_states):
            aliases[len(args)] = 1 + i
            args.append(st)
    n_alias = len(aliases)
    in_specs += [pl.BlockSpec(memory_space=pl.ANY)] * n_alias
    out_specs = [pl.BlockSpec((c, D_MIX), lambda b, k: (blk0 + b * nc + k, 0))]
    out_specs += [out_state_spec(s) for s in state_shapes]
    out_shape = [jax.ShapeDtypeStruct((total_rows, D_MIX), BF16)]
    out_shape += [jax.ShapeDtypeStruct((DEPTH, nb) + s, F32) for s in state_shapes]
    body = functools.partial(_mixer_kernel, nseq, seq_len, nlev, nmerge, nsquare, nc == 1, n_alias)
    return pl.pallas_call(
        body, grid=(nb // nseq, nc), in_specs=in_specs, out_specs=out_specs, out_shape=out_shape,
        input_output_aliases=aliases,
        scratch_shapes=[pltpu.VMEM((nseq, CONV_PAD + seq_len, CONV_CH), F32)],
        compiler_params=pltpu.CompilerParams(dimension_semantics=("parallel", "arbitrary"),
                                             vmem_limit_bytes=VMEM_LIMIT_BYTES),
        name=f"mixer_{nseq}x{seq_len}")(*args)


def _permute_w_in(w):
    o = CONV_CH
    dz = w[:, o:o + DN_V]; o += DN_V
    da = w[:, o:o + DN_HEADS]; o += DN_HEADS
    db = w[:, o:o + DN_HEADS]; o += DN_HEADS
    gq = w[:, o:o + GLA_QK]; o += GLA_QK
    gk = w[:, o:o + GLA_QK]; o += GLA_QK
    gv = w[:, o:o + GLA_V]; o += GLA_V
    glow = w[:, o:o + GLA_RANK]; o += GLA_RANK
    gg = w[:, o:o + GLA_V]; o += GLA_V
    mz = w[:, o:o + M2_DIM]; o += M2_DIM
    mdt = w[:, o:o + M2_HEADS]; o += M2_HEADS
    assert o == w.shape[1]
    pad = jnp.zeros((w.shape[0], LANES - SM_USED), w.dtype)
    return jnp.concatenate([w[:, :CONV_CH], dz, gq, gk, gv, gg, mz, da, mdt, db, glow, pad], axis=1)


def _small_lane_row(dn_vals, m2_vals):
    pad = jnp.zeros((LANES - SM_DB,), F32)
    return jnp.concatenate([dn_vals.astype(F32), m2_vals.astype(F32), pad]).reshape(1, LANES)


def _ssm_to_pairs(s):
    t = jnp.swapaxes(s, -1, -2)
    return t.reshape(t.shape[:-3] + (M2_PAIRS, 2 * M2_HEADDIM, M2_STATE))


def _ssm_from_pairs(s):
    t = s.reshape(s.shape[:-3] + (M2_HEADS, M2_HEADDIM, M2_STATE))
    return jnp.swapaxes(t, -1, -2)


def kernel(x_prompt, x_sample, state_conv, state_delta, state_gla, state_ssm, ffn1_norm, ffn1_w_gu, ffn1_w_down, mix_norm, w_in, conv_w, conv_b, dn_a_log, dn_dt_bias, dn_norm, gla_w_up, gla_b_up, gla_norm, m2_a_log, m2_dt_bias, m2_d, m2_norm, w_out, ffn2_norm, ffn2_w_gu, ffn2_w_down, final_norm):
    bp, sp, _ = x_prompt.shape
    bs, ss, _ = x_sample.shape
    tp, ts = bp * sp, bs * ss
    assert tp % DENSE_TILE == 0 and ts % DENSE_TILE == 0 and sp % PROMPT_CHUNK == 0
    assert bs % SAMPLE_SEQS_PER_STEP == 0 and tp % (SAMPLE_SEQS_PER_STEP * ss) == 0 and ss % SUBLANES == 0

    zero_conv = jnp.zeros((1, bp, CONV_TAIL, CONV_CH), F32)
    zero_dn = jnp.zeros((1, bp, DN_HEADS, DN_DK, DN_DV), F32)
    zero_gla = jnp.zeros((1, bp, GLA_HEADS, GLA_DK, GLA_DV), F32)
    zero_ssm = jnp.zeros((1, bp, M2_PAIRS, 2 * M2_HEADDIM, M2_STATE), F32)
    ssm_pairs = _ssm_to_pairs(state_ssm)

    xs = [x_prompt.reshape(tp, D_MODEL), x_sample.reshape(ts, D_MODEL)]
    st_p = st_s = None
    for l in range(DEPTH):
        wg1, wu1 = ffn1_w_gu[l, :, :D_FF].astype(BF16), ffn1_w_gu[l, :, D_FF:].astype(BF16)
        wg2, wu2 = ffn2_w_gu[l, :, :D_FF].astype(BF16), ffn2_w_gu[l, :, D_FF:].astype(BF16)
        params = (conv_w[l], conv_b[l].reshape(1, CONV_CH),
                  _small_lane_row(dn_a_log[l], m2_a_log[l]), _small_lane_row(dn_dt_bias[l], m2_dt_bias[l]),
                  dn_norm[l].reshape(1, DN_DV),
                  gla_w_up[l].astype(BF16), gla_b_up[l].reshape(1, GLA_QK), gla_norm[l].reshape(1, GLA_DV),
                  jnp.repeat(m2_d[l], M2_HEADDIM).reshape(1, M2_DIM), m2_norm[l].reshape(1, M2_DIM))

        x = _ffn(xs, ffn1_norm[l], wg1, wu1, ffn1_w_down[l].astype(BF16))
        proj = _inproj(x, mix_norm[l], _permute_w_in(w_in[l]).astype(BF16))
        mix, *st_p = _mixer(proj, 0, bp, sp, 1, PROMPT_CHUNK, l, zero_conv, zero_dn, zero_gla, zero_ssm, params,
                            tp + ts, prev_states=st_p)
        mix, *st_s = _mixer(proj, tp, bs, ss, SAMPLE_SEQS_PER_STEP, ss, l,
                            state_conv, state_delta, state_gla, ssm_pairs, params,
                            tp + ts, prev_mix=mix, prev_states=st_s)
        x = _outproj(x, mix, w_out[l].astype(BF16))
        if l < DEPTH - 1:
            xs = [_ffn([x], ffn2_norm[l], wg2, wu2, ffn2_w_down[l].astype(BF16))]
        else:
            y_prompt, y_sample = _ffn([x], ffn2_norm[l], wg2, wu2, ffn2_w_down[l].astype(BF16),
                                      final_gain=final_norm, split_rows=tp)

    return (y_prompt.reshape(bp, sp, D_MODEL), y_sample.reshape(bs, ss, D_MODEL),
            st_p[0], st_p[1], st_p[2], _ssm_from_pairs(st_p[3]),
            st_s[0], st_s[1], st_s[2], _ssm_from_pairs(st_s[3]))
```

```python
import functools
import math

import jax
import jax.numpy as jnp
import numpy as np
from jax import lax
from jax.experimental import pallas as pl
from jax.experimental.pallas import tpu as pltpu

F32 = jnp.float32
BF16 = jnp.bfloat16
LOG2E = math.log2(math.e)

D_MODEL = 1024
DEPTH = 2
D_MIX = 2 * D_MODEL
D_FF = 2816
CONV_W = 4
DN_HEADS = 4
DN_DK = 128
DN_DV = 128
GLA_HEADS = 4
GLA_DK = 64
GLA_DV = 128
GLA_RANK = 16
GLA_NORMALIZER = 16.0
M2_DIM = 1024
M2_HEADDIM = 64
M2_HEADS = 16
M2_STATE = 128
M2_GROUPS = 2
M2_PAIRS = M2_HEADS // 2
M2_PAIRS_PER_GROUP = M2_PAIRS // M2_GROUPS
EPS = 1e-6
LANES = 128
SUBLANES = 8

DN_QK = DN_HEADS * DN_DK
DN_V = DN_HEADS * DN_DV
GLA_QK = GLA_HEADS * GLA_DK
GLA_V = GLA_HEADS * GLA_DV
M2_BC = M2_GROUPS * M2_STATE
CONV_CH = 3 * 512 + M2_DIM + 2 * M2_BC

OFF_DQ = 0
OFF_DK = OFF_DQ + DN_QK
OFF_DV = OFF_DK + DN_QK
OFF_MX = OFF_DV + DN_V
OFF_MB = OFF_MX + M2_DIM
OFF_MC = OFF_MB + M2_BC
OFF_DZ = CONV_CH
OFF_GQ = OFF_DZ + DN_V
OFF_GK = OFF_GQ + GLA_QK
OFF_GV = OFF_GK + GLA_QK
OFF_GG = OFF_GV + GLA_V
OFF_MZ = OFF_GG + GLA_V
OFF_SMALL = OFF_MZ + M2_DIM
SM_DA = 0
SM_MDT = SM_DA + DN_HEADS
SM_DB = SM_MDT + M2_HEADS
SM_GLOW = SM_DB + DN_HEADS
SM_USED = SM_GLOW + GLA_RANK
PROJ_COLS = OFF_SMALL + LANES

CONV_PAD = SUBLANES
CONV_TAIL = CONV_W - 1
NEUMANN_BLOCK = 16
VMEM_LIMIT_BYTES = 56 * 1024 * 1024
PROMPT_CHUNK = 128
SAMPLE_SEQS_PER_STEP = 8
DENSE_TILE = 256
MASKED = -1e30
NT_DIMS = (((1,), (1,)), ((), ()))
TN_DIMS = (((0,), (0,)), ((), ()))


def _dot(a, b):
    return jnp.dot(a.astype(BF16), b.astype(BF16), preferred_element_type=F32)


def _dot_nt(a, b):
    return lax.dot_general(a.astype(BF16), b.astype(BF16), NT_DIMS, preferred_element_type=F32)


def _dot_tn(a, b):
    return lax.dot_general(a.astype(BF16), b.astype(BF16), TN_DIMS, preferred_element_type=F32)


def _split3(x):
    hi = x.astype(BF16)
    r1 = x - hi.astype(F32)
    mid = r1.astype(BF16)
    lo = (r1 - mid.astype(F32)).astype(BF16)
    return hi, mid, lo


def _sum_dots(pieces, fn):
    acc = None
    for piece in pieces:
        part = fn(piece)
        acc = part if acc is None else acc + part
    return acc


def _dot_exact_lhs(m, pieces):
    return _sum_dots(pieces, lambda p: jnp.dot(m, p, preferred_element_type=F32))


def _dot_exact_rhs(pieces, m):
    return _sum_dots(pieces, lambda p: jnp.dot(p, m, preferred_element_type=F32))


def _dot_exact_tn(pieces, m):
    return _sum_dots(pieces, lambda p: lax.dot_general(p, m, TN_DIMS, preferred_element_type=F32))


def _silu(x):
    return x * jax.nn.sigmoid(x)


def _softplus(x):
    return jnp.maximum(x, 0.0) + jnp.log(1.0 + jnp.exp(-jnp.abs(x)))


def _rms(x, gain):
    return x * lax.rsqrt(jnp.mean(x * x, axis=-1, keepdims=True) + EPS) * gain


def _ffn_kernel(split_tile, final_norm, with_outproj, *refs):
    refs = list(refs)
    two_in = split_tile is not None and not final_norm
    two_out = split_tile is not None and final_norm
    x_refs = [refs.pop(0) for _ in range(2 if two_in else 1)]
    mix_ref, wout_ref = (refs.pop(0), refs.pop(0)) if with_outproj else (None, None)
    g_ref, wgu_ref, wd_ref = (refs.pop(0) for _ in range(3))
    fg_ref = refs.pop(0) if final_norm else None
    o_refs = refs
    i = pl.program_id(0)
    x = jnp.where(i < split_tile, x_refs[0][...], x_refs[1][...]) if two_in else x_refs[0][...]
    if with_outproj:
        x = x + jnp.dot(mix_ref[...], wout_ref[...], preferred_element_type=F32)
    xn = _rms(x, g_ref[...]).astype(BF16)
    gate = jnp.dot(xn, wgu_ref[:, :D_FF], preferred_element_type=F32)
    up = jnp.dot(xn, wgu_ref[:, D_FF:], preferred_element_type=F32)
    h = (_silu(gate) * up).astype(BF16)
    y = x + 0.5 * jnp.dot(h, wd_ref[...], preferred_element_type=F32)
    if final_norm:
        y = _rms(y, fg_ref[...])
    if two_out:
        @pl.when(i < split_tile)
        def _():
            o_refs[0][...] = y

        @pl.when(i >= split_tile)
        def _():
            o_refs[1][...] = y
    else:
        o_refs[0][...] = y


def _inproj_kernel(x_ref, g_ref, w_ref, o_ref):
    xn = _rms(x_ref[...], g_ref[...]).astype(BF16)
    o_ref[...] = jnp.dot(xn, w_ref[...], preferred_element_type=F32)


def _resident(shape):
    nd = len(shape)
    return pl.BlockSpec(shape, lambda *_: (0,) * nd, pipeline_mode=pl.Buffered(1))


def _layer_resident(arr, layer):
    zeros = (0,) * (arr.ndim - 1)
    return pl.BlockSpec((None,) + arr.shape[1:], lambda *_: (layer,) + zeros, pipeline_mode=pl.Buffered(1))


def _ffn(xs, layer, gains, wgu, wd, final_gain=None, split_rows=None, mix=None, wout=None):
    t = sum(x.shape[0] for x in xs)
    two_in = len(xs) == 2
    two_out = final_gain is not None and split_rows is not None
    split_tile = None
    if two_in or two_out:
        split_tile = (xs[0].shape[0] if two_in else split_rows) // DENSE_TILE
    tile = pl.BlockSpec((DENSE_TILE, D_MODEL), lambda i: (i, 0))
    first = pl.BlockSpec((DENSE_TILE, D_MODEL), lambda i: (jnp.minimum(i, split_tile - 1), 0))
    second = pl.BlockSpec((DENSE_TILE, D_MODEL), lambda i: (jnp.maximum(i - split_tile, 0), 0))
    in_specs = [first, second] if two_in else [tile]
    args = list(xs)
    if mix is not None:
        in_specs += [pl.BlockSpec((DENSE_TILE, D_MIX), lambda i: (i, 0)), _layer_resident(wout, layer)]
        args += [mix, wout]
    in_specs += [_layer_resident(gains, layer), _layer_resident(wgu, layer), _layer_resident(wd, layer)]
    args += [gains, wgu, wd]
    if final_gain is not None:
        in_specs.append(_resident((1, D_MODEL)))
        args.append(final_gain.reshape(1, D_MODEL))
    if two_out:
        out_specs = [first, second]
        out_shape = [jax.ShapeDtypeStruct((split_rows, D_MODEL), F32),
                     jax.ShapeDtypeStruct((t - split_rows, D_MODEL), F32)]
    else:
        out_specs, out_shape = tile, jax.ShapeDtypeStruct((t, D_MODEL), F32)
    return pl.pallas_call(
        functools.partial(_ffn_kernel, split_tile, final_gain is not None, mix is not None),
        grid=(t // DENSE_TILE,), in_specs=in_specs, out_specs=out_specs, out_shape=out_shape,
        compiler_params=pltpu.CompilerParams(dimension_semantics=("arbitrary",),
                                             vmem_limit_bytes=VMEM_LIMIT_BYTES),
        name="ffn" if final_gain is None else "ffn_final_norm")(*args)


def _inproj(x, layer, gains, w):
    t = x.shape[0]
    return pl.pallas_call(
        _inproj_kernel, grid=(t // DENSE_TILE,),
        in_specs=[pl.BlockSpec((DENSE_TILE, D_MODEL), lambda i: (i, 0)), _layer_resident(gains, layer),
                  _layer_resident(w, layer)],
        out_specs=pl.BlockSpec((DENSE_TILE, PROJ_COLS), lambda i: (i, 0)),
        out_shape=jax.ShapeDtypeStruct((t, PROJ_COLS), F32),
        compiler_params=pltpu.CompilerParams(dimension_semantics=("parallel",), vmem_limit_bytes=VMEM_LIMIT_BYTES),
        name="inproj")(x, gains, w)


CONST_NAMES = ("tril", "negi", "blk", "merge", "sel", "lv", "expand", "bc_dn", "bc_beta", "bc_ssd")


def _mixer_constants(nseq, seq_len):
    c = nseq * seq_len
    idx = np.arange(c)
    seq, pos = idx // seq_len, idx % seq_len
    same = seq[:, None] == seq[None, :]
    pi, pj = pos[:, None], pos[None, :]
    causal = same & (pi >= pj)
    base = min(seq_len, NEUMANN_BLOCK)
    merges = []
    s = base
    while s < seq_len:
        merges.append(same & (pi // (2 * s) == pj // (2 * s)) & (pi // s != pj // s))
        s *= 2
    levels = []
    m = seq_len // 2
    while m >= 1:
        levels.append(m)
        m //= 2
    sel = np.zeros((len(levels), c, c), np.float32)
    lv = np.full((c, c), -1, np.int32)
    xor = pi ^ pj
    for k, m in enumerate(levels):
        mid = seq * seq_len + (pos // (2 * m)) * (2 * m) + (m - 1)
        sel[k, idx, mid] = 1.0
        lv[same & (pi > pj) & (xor >= m) & (xor < 2 * m)] = k
    lv[idx, idx] = len(levels)
    expand = np.zeros((LANES, M2_DIM), np.float32)
    bc_ssd = np.zeros((LANES, M2_HEADS * LANES), np.float32)
    for h in range(M2_HEADS):
        expand[SM_MDT + h, h * M2_HEADDIM:(h + 1) * M2_HEADDIM] = 1.0
        bc_ssd[SM_MDT + h, h * LANES:(h + 1) * LANES] = 1.0
    bc_dn = np.zeros((LANES, DN_HEADS * LANES), np.float32)
    bc_beta = np.zeros((LANES, DN_HEADS * LANES), np.float32)
    for h in range(DN_HEADS):
        bc_dn[SM_DA + h, h * LANES:(h + 1) * LANES] = 1.0
        bc_beta[SM_DB + h, h * LANES:(h + 1) * LANES] = 1.0
    consts = dict(
        tril=jnp.asarray(causal, BF16),
        negi=jnp.asarray(np.where(causal, 0.0, MASKED), F32),
        blk=jnp.asarray(same & (pi // base == pj // base), F32),
        merge=jnp.asarray(np.stack(merges) if merges else np.zeros((1, SUBLANES, LANES)), F32),
        sel=jnp.asarray(sel.reshape(len(levels) * c, c), BF16),
        lv=jnp.asarray(lv),
        expand=jnp.asarray(expand, BF16),
        bc_dn=jnp.asarray(bc_dn, BF16),
        bc_beta=jnp.asarray(bc_beta, BF16),
        bc_ssd=jnp.asarray(bc_ssd, BF16),
    )
    nsquare = int(math.log2(base)) - 1
    return consts, len(levels), len(merges), nsquare


def _round_robin(*staged):
    pending = list(staged)
    while pending:
        for item in list(pending):
            g, per_turn = item
            try:
                for _ in range(per_turn):
                    next(g)
            except StopIteration:
                pending.remove(item)


def _mixer_kernel(nseq, seq_len, nlev, nmerge, nsquare, single_chunk, n_alias,
                  proj_ref, conv0_ref, dn0_ref, gla0_ref, ssm0_ref,
                  convw_ref, convb_ref, alog_ref, dtb_ref, dn_norm_ref, gla_wup_ref, gla_bup_ref, gla_norm_ref,
                  m2_d_ref, m2_norm_ref,
                  tril_ref, negi_ref, blk_ref, merge_ref, sel_ref, lv_ref, expand_ref, bc_dn_ref, bc_beta_ref,
                  bc_ssd_ref, *rest):
    mix_ref, convn_ref, dnn_ref, glan_ref, ssmn_ref, cbuf = rest[n_alias:]
    c = nseq * seq_len
    chunk = pl.program_id(1)
    last_chunk = pl.num_programs(1) - 1
    seq_rows = [slice(b * seq_len, (b + 1) * seq_len) for b in range(nseq)]

    def at_first_chunk(fn):
        if single_chunk:
            fn()
        else:
            pl.when(chunk == 0)(fn)

    def at_last_chunk(fn):
        if single_chunk:
            fn()
        else:
            pl.when(chunk == last_chunk)(fn)

    @at_first_chunk
    def _():
        cbuf[:, pl.ds(CONV_PAD - CONV_TAIL, CONV_TAIL), :] = conv0_ref[...]
        if not single_chunk:
            dnn_ref[...] = dn0_ref[...]
            glan_ref[...] = gla0_ref[...]
            ssmn_ref[...] = ssm0_ref[...]

    cbuf[:, pl.ds(CONV_PAD, seq_len), :] = proj_ref[:, 0:CONV_CH].reshape(nseq, seq_len, CONV_CH)

    def conv(lo, width):
        acc = convb_ref[:, lo:lo + width].reshape(1, 1, width)
        for w in range(CONV_W):
            tap = convw_ref[w:w + 1, lo:lo + width].reshape(1, 1, width)
            acc = acc + tap * cbuf[:, pl.ds(CONV_PAD - CONV_TAIL + w, seq_len), lo:lo + width]
        return _silu(acc).reshape(c, width)

    ones_sq = jnp.ones((LANES, LANES), BF16)

    def row_rsqrt(x, mult):
        return row_rsqrt_of_squares(x * x, mult)

    def row_rsqrt_of_squares(sq, mult):
        sums = jnp.dot(sq.astype(BF16), ones_sq, preferred_element_type=F32)
        return lax.rsqrt(sums * mult + EPS)

    def per_seq_last(x):
        parts = [jnp.broadcast_to(x[r.stop - 1:r.stop, :], (seq_len, x.shape[1])) for r in seq_rows]
        return parts[0] if nseq == 1 else jnp.concatenate(parts, axis=0)

    def cat_rows(parts):
        return parts[0] if nseq == 1 else jnp.concatenate(parts, axis=0)

    tril = tril_ref[...]
    negi = negi_ref[...]
    lv = lv_ref[...]
    diag = lv == nlev

    small = proj_ref[:, OFF_SMALL:OFF_SMALL + LANES]
    sp = _softplus(small + dtb_ref[...])
    g2 = (-LOG2E * jnp.exp(alog_ref[...])) * sp
    cum = _dot_exact_lhs(tril, _split3(g2))
    cum_t = cum.T
    last = per_seq_last(cum)
    e_cum = jnp.exp2(cum)
    e_rem = jnp.exp2(last - cum)
    e_last = jnp.exp2(last)
    cum_p, e_cum_p, e_rem_p = _split3(cum), _split3(e_cum), _split3(e_rem)

    def deltanet():
        bc_dn = bc_dn_ref[...]
        cum_col = _dot_exact_rhs(cum_p, bc_dn)
        egc_col = _dot_exact_rhs(e_cum_p, bc_dn)
        erem_col = _dot_exact_rhs(e_rem_p, bc_dn)
        beta_col = _dot_exact_rhs(_split3(jax.nn.sigmoid(small)), bc_beta_ref[...])
        dq_all = conv(OFF_DQ, DN_QK)
        dk_all = conv(OFF_DK, DN_QK)
        dv_all = conv(OFF_DV, DN_V)
        yield
        dn = []
        for h in range(DN_HEADS):
            hs = slice(h * DN_DK, (h + 1) * DN_DK)
            dq, dk = dq_all[:, hs], dk_all[:, hs]
            st = dict(q=dq * (row_rsqrt(dq, 1.0) * (DN_DK ** -0.5)), k=dk * row_rsqrt(dk, 1.0),
                      v=dv_all[:, hs], beta=beta_col[:, hs], egc=egc_col[:, hs], erem=erem_col[:, hs])
            dn.append(st)
        yield
        for h, st in enumerate(dn):
            kb, qb = st["k"].astype(BF16), st["q"].astype(BF16)
            decay = jnp.exp2(cum_col[:, h * LANES:h * LANES + c] - cum_t[SM_DA + h:SM_DA + h + 1, :] + negi)
            kk = lax.dot_general(kb, kb, NT_DIMS, preferred_element_type=F32)
            qk = lax.dot_general(qb, kb, NT_DIMS, preferred_element_type=F32)
            st["a"] = jnp.where(diag, 0.0, kk * decay * st["beta"][:, :c])
            st["qkd"] = qk * decay
        yield
        blk = blk_ref[...]
        for st in dn:
            st["p"] = st["a"] * blk
            st["n"] = -st["p"]
        for _ in range(nsquare):
            for st in dn:
                st["p"] = _dot(st["p"], st["p"])
            yield
            for st in dn:
                st["n"] = st["n"] + st["p"] + _dot(st["n"], st["p"])
            yield
        for lvl in range(nmerge):
            emask = merge_ref[lvl]
            for st in dn:
                e = st["a"] * emask
                st["y"] = e + _dot(e, st["n"])
            yield
            for st in dn:
                st["n"] = st["n"] - st["y"] - _dot(st["n"], st["y"])
            yield
        for st in dn:
            rhs = jnp.concatenate([st["k"] * (st["beta"] * st["egc"]), st["v"] * st["beta"]], axis=1)
            st["sol"] = rhs + _dot(st["n"], rhs)
        yield
        for h, st in enumerate(dn):
            sol = st["sol"]
            w_mat, u_mat = sol[:, :DN_DK], sol[:, DN_DK:]
            qe = st["q"] * st["egc"]
            v_parts, o_parts, states = [], [], []
            for b, r in enumerate(seq_rows):
                s = dn0_ref[b, h] if single_chunk else dnn_ref[b, h]
                both = _dot(jnp.concatenate([w_mat[r], qe[r]], axis=0), s)
                v_parts.append(u_mat[r] - both[:seq_len])
                o_parts.append(both[seq_len:])
                states.append(s)
            st["v_new"], st["o"], st["s"] = cat_rows(v_parts), cat_rows(o_parts), states
        yield
        for h, st in enumerate(dn):
            v_new = st["v_new"]
            o = st["o"] + _dot(st["qkd"], v_new)
            kdec = st["k"] * st["erem"]
            lane = SM_DA + h
            for b, r in enumerate(seq_rows):
                dnn_ref[b, h] = (e_last[r.start:r.start + 1, lane:lane + 1] * st["s"][b]
                                 + _dot_tn(kdec[r], v_new[r]))
            st["o"] = o
        yield
        for h, st in enumerate(dn):
            o = st["o"]
            z = proj_ref[:, OFF_DZ + h * DN_DV:OFF_DZ + (h + 1) * DN_DV]
            o_n = o * row_rsqrt(o, 1.0 / DN_DV) * dn_norm_ref[...]
            mix_ref[:, h * DN_DV:(h + 1) * DN_DV] = (o_n * _silu(z)).astype(BF16)

    def gla():
        glow = small[:, SM_GLOW:SM_GLOW + GLA_RANK]
        zg = _dot(glow, gla_wup_ref[...]) + gla_bup_ref[...]
        gk2 = _softplus(-zg) * (-LOG2E / GLA_NORMALIZER)
        gcum = _dot_exact_lhs(tril, _split3(gk2))
        yield
        gq = proj_ref[:, OFF_GQ:OFF_GQ + GLA_QK] * (GLA_DK ** -0.5)
        gkk = proj_ref[:, OFF_GK:OFF_GK + GLA_QK]
        gq_b, gkk_b = gq.astype(BF16), gkk.astype(BF16)
        gcum_p = _split3(gcum)
        gmid_all = _dot_exact_lhs(sel_ref[...], gcum_p)
        yield
        level_ops = []
        for lev in range(nlev):
            e = jnp.exp2(-jnp.abs(gcum - gmid_all[lev * c:(lev + 1) * c, :]))
            level_ops.append(((gq * e).astype(BF16), (gkk * e).astype(BF16), lv == lev))
            if lev % 2 == 1:
                yield
        glast = per_seq_last(gcum)
        q_inter = gq * jnp.exp2(gcum)
        k_dec = gkk * jnp.exp2(glast - gcum)
        eye_dk = (lax.broadcasted_iota(jnp.int32, (GLA_DK, GLA_DK), 0)
                  == lax.broadcasted_iota(jnp.int32, (GLA_DK, GLA_DK), 1))
        ones_dk = jnp.ones((GLA_DK, GLA_DV), BF16)
        yield
        for h in range(GLA_HEADS):
            hs = slice(h * GLA_DK, (h + 1) * GLA_DK)
            v = proj_ref[:, OFF_GV + h * GLA_DV:OFF_GV + (h + 1) * GLA_DV]
            scores = jnp.where(diag, lax.dot_general(gq_b[:, hs], gkk_b[:, hs], NT_DIMS,
                                                     preferred_element_type=F32), 0.0)
            for q_t, k_t, mask in level_ops:
                s_lev = lax.dot_general(q_t[:, hs], k_t[:, hs], NT_DIMS, preferred_element_type=F32)
                scores = jnp.where(mask, s_lev, scores)
            yield
            o_parts = []
            for b, r in enumerate(seq_rows):
                s = gla0_ref[b, h] if single_chunk else glan_ref[b, h]
                o_parts.append(_dot(q_inter[r, hs], s))
                lam = jnp.where(eye_dk, jnp.exp2(glast[r.start:r.start + 1, hs]), 0.0)
                keep = _dot_exact_rhs(_split3(lam), ones_dk)
                glan_ref[b, h] = keep * s + _dot_tn(k_dec[r, hs], v[r])
            o = cat_rows(o_parts) + _dot(scores, v)
            yield
            z = proj_ref[:, OFF_GG + h * GLA_DV:OFF_GG + (h + 1) * GLA_DV]
            o_n = o * row_rsqrt(o, 1.0 / GLA_DV) * gla_norm_ref[...]
            mix_ref[:, DN_V + h * GLA_DV:DN_V + (h + 1) * GLA_DV] = (o_n * _silu(z)).astype(BF16)

    def ssd():
        expand = expand_ref[...]
        sp_p = _split3(sp)
        dt_x = _dot_exact_rhs(sp_p, expand)
        ecum_x = _dot_exact_rhs(e_cum_p, expand)
        dtrem_x = _dot_exact_rhs(_split3(sp * e_rem), expand)
        yield
        cum_cols = _dot_exact_rhs(cum_p, bc_ssd_ref[...])
        lane_lo = lax.broadcasted_iota(jnp.int32, (c, LANES), 1) < M2_HEADDIM
        row_lo = lax.broadcasted_iota(jnp.int32, (LANES, LANES), 0) < M2_HEADDIM
        yield
        m2_off = DN_V + GLA_V
        total = jnp.zeros((c, LANES), F32)
        gated = []
        for g in range(M2_GROUPS):
            bm = conv(OFF_MB + g * M2_STATE, M2_STATE)
            cm = conv(OFF_MC + g * M2_STATE, M2_STATE)
            bm_b, cm_b = bm.astype(BF16), cm.astype(BF16)
            cb = lax.dot_general(cm_b, bm_b, NT_DIMS, preferred_element_type=F32)
            yield
            for pp in range(M2_PAIRS_PER_GROUP):
                p = g * M2_PAIRS_PER_GROUP + pp
                ps = slice(p * LANES, (p + 1) * LANES)
                ha, hb = SM_MDT + 2 * p, SM_MDT + 2 * p + 1
                xh = conv(OFF_MX + p * LANES, LANES)
                x_dt = xh * dt_x[:, ps]
                col_a, col_b = 2 * p * LANES, (2 * p + 1) * LANES
                dec_a = jnp.exp2(cum_cols[:, col_a:col_a + c] - cum_t[ha:ha + 1, :] + negi)
                dec_b = jnp.exp2(cum_cols[:, col_b:col_b + c] - cum_t[hb:hb + 1, :] + negi)
                m_pair = jnp.concatenate([cb * dec_a, cb * dec_b], axis=1)
                x_bd = jnp.concatenate([jnp.where(lane_lo, x_dt, 0.0), jnp.where(lane_lo, 0.0, x_dt)], axis=0)
                x_rem = (xh * dtrem_x[:, ps]).astype(BF16)
                inter = []
                for b, r in enumerate(seq_rows):
                    s = ssm0_ref[b, p] if single_chunk else ssmn_ref[b, p]
                    inter.append(lax.dot_general(cm_b[r], s.astype(BF16), NT_DIMS, preferred_element_type=F32))
                    keep = jnp.where(row_lo, e_last[r.start:r.start + 1, ha:ha + 1],
                                     e_last[r.start:r.start + 1, hb:hb + 1])
                    ssmn_ref[b, p] = keep * s + lax.dot_general(x_rem[r], bm_b[r], TN_DIMS,
                                                                preferred_element_type=F32)
                y = _dot(m_pair, x_bd) + cat_rows(inter) * ecum_x[:, ps] + m2_d_ref[:, ps] * xh
                z = proj_ref[:, OFF_MZ + p * LANES:OFF_MZ + (p + 1) * LANES]
                yz = y * _silu(z)
                total = total + yz * yz
                gated.append(yz)
                yield
        scale = row_rsqrt_of_squares(total, 1.0 / M2_DIM)
        for p, yz in enumerate(gated):
            ps = slice(p * LANES, (p + 1) * LANES)
            mix_ref[:, m2_off + p * LANES:m2_off + (p + 1) * LANES] = (yz * scale * m2_norm_ref[:, ps]).astype(BF16)

    _round_robin((ssd(), 1), (gla(), 1), (deltanet(), 1))

    @at_last_chunk
    def _():
        convn_ref[...] = cbuf[:, pl.ds(CONV_PAD + seq_len - CONV_TAIL, CONV_TAIL), :]

    if not single_chunk:
        cbuf[:, pl.ds(0, CONV_PAD), :] = cbuf[:, pl.ds(seq_len, CONV_PAD), :]


def _mixer(proj, row_start, nb, seq, nseq, seq_len, layer, conv0, dn0, gla0, ssm0, params,
           total_rows, prev_mix=None, prev_states=None):
    c = nseq * seq_len
    nc = seq // seq_len
    blk0 = row_start // c
    lsel = layer if conv0.shape[0] > 1 else 0
    consts, nlev, nmerge, nsquare = _mixer_constants(nseq, seq_len)
    const_args = [consts[k] for k in CONST_NAMES]

    def state_spec(shape):
        zeros = (0,) * len(shape)
        return pl.BlockSpec((None, nseq) + shape, lambda b, k: (lsel, b) + zeros)

    def out_state_spec(shape):
        zeros = (0,) * len(shape)
        return pl.BlockSpec((None, nseq) + shape, lambda b, k: (layer, b) + zeros)

    conv_shape = (CONV_TAIL, CONV_CH)
    dn_shape = (DN_HEADS, DN_DK, DN_DV)
    gla_shape = (GLA_HEADS, GLA_DK, GLA_DV)
    ssm_shape = (M2_PAIRS, 2 * M2_HEADDIM, M2_STATE)
    state_shapes = (conv_shape, dn_shape, gla_shape, ssm_shape)
    in_specs = [pl.BlockSpec((c, PROJ_COLS), lambda b, k: (blk0 + b * nc + k, 0))]
    in_specs += [state_spec(s) for s in state_shapes]
    in_specs += [_resident(p.shape) for p in params]
    in_specs += [_resident(a.shape) for a in const_args]
    args = [proj, conv0, dn0, gla0, ssm0, *params, *const_args]
    aliases = {}
    if prev_mix is not None:
        aliases[len(args)] = 0
        args.append(prev_mix)
    if prev_states is not None:
        for i, st in enumerate(prev_states):
            aliases[len(args)] = 1 + i
            args.append(st)
    n_alias = len(aliases)
    in_specs += [pl.BlockSpec(memory_space=pl.ANY)] * n_alias
    out_specs = [pl.BlockSpec((c, D_MIX), lambda b, k: (blk0 + b * nc + k, 0))]
    out_specs += [out_state_spec(s) for s in state_shapes]
    out_shape = [jax.ShapeDtypeStruct((total_rows, D_MIX), BF16)]
    out_shape += [jax.ShapeDtypeStruct((DEPTH, nb) + s, F32) for s in state_shapes]
    body = functools.partial(_mixer_kernel, nseq, seq_len, nlev, nmerge, nsquare, nc == 1, n_alias)
    return pl.pallas_call(
        body, grid=(nb // nseq, nc), in_specs=in_specs, out_specs=out_specs, out_shape=out_shape,
        input_output_aliases=aliases,
        scratch_shapes=[pltpu.VMEM((nseq, CONV_PAD + seq_len, CONV_CH), F32)],
        compiler_params=pltpu.CompilerParams(dimension_semantics=("parallel", "arbitrary"),
                                             vmem_limit_bytes=VMEM_LIMIT_BYTES),
        name=f"mixer_{nseq}x{seq_len}")(*args)


def _permute_w_in(w):
    o = CONV_CH
    dz = w[..., o:o + DN_V]; o += DN_V
    da = w[..., o:o + DN_HEADS]; o += DN_HEADS
    db = w[..., o:o + DN_HEADS]; o += DN_HEADS
    gq = w[..., o:o + GLA_QK]; o += GLA_QK
    gk = w[..., o:o + GLA_QK]; o += GLA_QK
    gv = w[..., o:o + GLA_V]; o += GLA_V
    glow = w[..., o:o + GLA_RANK]; o += GLA_RANK
    gg = w[..., o:o + GLA_V]; o += GLA_V
    mz = w[..., o:o + M2_DIM]; o += M2_DIM
    mdt = w[..., o:o + M2_HEADS]; o += M2_HEADS
    assert o == w.shape[-1]
    pad = jnp.zeros(w.shape[:-1] + (LANES - SM_USED,), w.dtype)
    return jnp.concatenate([w[..., :CONV_CH], dz, gq, gk, gv, gg, mz, da, mdt, db, glow, pad], axis=-1)


def _small_lane_row(dn_vals, m2_vals):
    pad = jnp.zeros((LANES - SM_DB,), F32)
    return jnp.concatenate([dn_vals.astype(F32), m2_vals.astype(F32), pad]).reshape(1, LANES)


def _ssm_to_pairs(s):
    t = jnp.swapaxes(s, -1, -2)
    return t.reshape(t.shape[:-3] + (M2_PAIRS, 2 * M2_HEADDIM, M2_STATE))


def _ssm_from_pairs(s):
    t = s.reshape(s.shape[:-3] + (M2_HEADS, M2_HEADDIM, M2_STATE))
    return jnp.swapaxes(t, -1, -2)


def kernel(x_prompt, x_sample, state_conv, state_delta, state_gla, state_ssm, ffn1_norm, ffn1_w_gu, ffn1_w_down, mix_norm, w_in, conv_w, conv_b, dn_a_log, dn_dt_bias, dn_norm, gla_w_up, gla_b_up, gla_norm, m2_a_log, m2_dt_bias, m2_d, m2_norm, w_out, ffn2_norm, ffn2_w_gu, ffn2_w_down, final_norm):
    bp, sp, _ = x_prompt.shape
    bs, ss, _ = x_sample.shape
    tp, ts = bp * sp, bs * ss
    assert tp % DENSE_TILE == 0 and ts % DENSE_TILE == 0 and sp % PROMPT_CHUNK == 0
    assert bs % SAMPLE_SEQS_PER_STEP == 0 and tp % (SAMPLE_SEQS_PER_STEP * ss) == 0 and ss % SUBLANES == 0

    zero_conv = jnp.zeros((1, bp, CONV_TAIL, CONV_CH), F32)
    zero_dn = jnp.zeros((1, bp, DN_HEADS, DN_DK, DN_DV), F32)
    zero_gla = jnp.zeros((1, bp, GLA_HEADS, GLA_DK, GLA_DV), F32)
    zero_ssm = jnp.zeros((1, bp, M2_PAIRS, 2 * M2_HEADDIM, M2_STATE), F32)
    ssm_pairs = _ssm_to_pairs(state_ssm)

    wgu1, wd1 = ffn1_w_gu.astype(BF16), ffn1_w_down.astype(BF16)
    wgu2, wd2 = ffn2_w_gu.astype(BF16), ffn2_w_down.astype(BF16)
    win, wout = _permute_w_in(w_in.astype(BF16)), w_out.astype(BF16)
    g_ffn1, g_mix, g_ffn2 = (g.reshape(DEPTH, 1, D_MODEL) for g in (ffn1_norm, mix_norm, ffn2_norm))

    xs = [x_prompt.reshape(tp, D_MODEL), x_sample.reshape(ts, D_MODEL)]
    st_p = st_s = None
    for l in range(DEPTH):
        params = (conv_w[l], conv_b[l].reshape(1, CONV_CH),
                  _small_lane_row(dn_a_log[l], m2_a_log[l]), _small_lane_row(dn_dt_bias[l], m2_dt_bias[l]),
                  dn_norm[l].reshape(1, DN_DV),
                  gla_w_up[l].astype(BF16), gla_b_up[l].reshape(1, GLA_QK), gla_norm[l].reshape(1, GLA_DV),
                  jnp.repeat(m2_d[l], M2_HEADDIM).reshape(1, M2_DIM), m2_norm[l].reshape(1, M2_DIM))

        x = _ffn(xs, l, g_ffn1, wgu1, wd1)
        proj = _inproj(x, l, g_mix, win)
        mix, *st_p = _mixer(proj, 0, bp, sp, 1, PROMPT_CHUNK, l, zero_conv, zero_dn, zero_gla, zero_ssm, params,
                            tp + ts, prev_states=st_p)
        mix, *st_s = _mixer(proj, tp, bs, ss, SAMPLE_SEQS_PER_STEP, ss, l,
                            state_conv, state_delta, state_gla, ssm_pairs, params,
                            tp + ts, prev_mix=mix, prev_states=st_s)
        if l < DEPTH - 1:
            xs = [_ffn([x], l, g_ffn2, wgu2, wd2, mix=mix, wout=wout)]
        else:
            y_prompt, y_sample = _ffn([x], l, g_ffn2, wgu2, wd2, final_gain=final_norm, split_rows=tp,
                                      mix=mix, wout=wout)

    return (y_prompt.reshape(bp, sp, D_MODEL), y_sample.reshape(bs, ss, D_MODEL),
            st_p[0], st_p[1], st_p[2], _ssm_from_pairs(st_p[3]),
            st_s[0], st_s[1], st_s[2], _ssm_from_pairs(st_s[3]))
```

```python
import functools
import math

import jax
import jax.numpy as jnp
import numpy as np
from jax import lax
from jax.experimental import pallas as pl
from jax.experimental.pallas import tpu as pltpu

F32 = jnp.float32
BF16 = jnp.bfloat16
LOG2E = math.log2(math.e)

D_MODEL = 1024
DEPTH = 2
D_MIX = 2 * D_MODEL
D_FF = 2816
CONV_W = 4
DN_HEADS = 4
DN_DK = 128
DN_DV = 128
GLA_HEADS = 4
GLA_DK = 64
GLA_DV = 128
GLA_RANK = 16
GLA_NORMALIZER = 16.0
M2_DIM = 1024
M2_HEADDIM = 64
M2_HEADS = 16
M2_STATE = 128
M2_GROUPS = 2
M2_PAIRS = M2_HEADS // 2
M2_PAIRS_PER_GROUP = M2_PAIRS // M2_GROUPS
EPS = 1e-6
LANES = 128
SUBLANES = 8

DN_QK = DN_HEADS * DN_DK
DN_V = DN_HEADS * DN_DV
GLA_QK = GLA_HEADS * GLA_DK
GLA_V = GLA_HEADS * GLA_DV
M2_BC = M2_GROUPS * M2_STATE
CONV_CH = 3 * 512 + M2_DIM + 2 * M2_BC

OFF_DQ = 0
OFF_DK = OFF_DQ + DN_QK
OFF_DV = OFF_DK + DN_QK
OFF_MX = OFF_DV + DN_V
OFF_MB = OFF_MX + M2_DIM
OFF_MC = OFF_MB + M2_BC
OFF_DZ = CONV_CH
OFF_GQ = OFF_DZ + DN_V
OFF_GK = OFF_GQ + GLA_QK
OFF_GV = OFF_GK + GLA_QK
OFF_GG = OFF_GV + GLA_V
OFF_MZ = OFF_GG + GLA_V
OFF_SMALL = OFF_MZ + M2_DIM
SM_DA = 0
SM_MDT = SM_DA + DN_HEADS
SM_DB = SM_MDT + M2_HEADS
SM_GLOW = SM_DB + DN_HEADS
SM_USED = SM_GLOW + GLA_RANK
PROJ_COLS = OFF_SMALL + LANES

CONV_PAD = SUBLANES
CONV_TAIL = CONV_W - 1
NEUMANN_BLOCK = 16
VMEM_LIMIT_BYTES = 56 * 1024 * 1024
PROMPT_CHUNK = 128
SAMPLE_SEQS_PER_STEP = 8
DENSE_TILE = 256
WEIGHT_CHUNK = 512
WEIGHT_ROWS = 256
MASKED = -1e30
NT_DIMS = (((1,), (1,)), ((), ()))
TN_DIMS = (((0,), (0,)), ((), ()))


def _dot(a, b):
    return jnp.dot(a.astype(BF16), b.astype(BF16), preferred_element_type=F32)


def _dot_nt(a, b):
    return lax.dot_general(a.astype(BF16), b.astype(BF16), NT_DIMS, preferred_element_type=F32)


def _dot_tn(a, b):
    return lax.dot_general(a.astype(BF16), b.astype(BF16), TN_DIMS, preferred_element_type=F32)


def _split3(x):
    hi = x.astype(BF16)
    r1 = x - hi.astype(F32)
    mid = r1.astype(BF16)
    lo = (r1 - mid.astype(F32)).astype(BF16)
    return hi, mid, lo


def _sum_dots(pieces, fn):
    acc = None
    for piece in pieces:
        part = fn(piece)
        acc = part if acc is None else acc + part
    return acc


def _dot_exact_lhs(m, pieces):
    return _sum_dots(pieces, lambda p: jnp.dot(m, p, preferred_element_type=F32))


def _dot_exact_rhs(pieces, m):
    return _sum_dots(pieces, lambda p: jnp.dot(p, m, preferred_element_type=F32))


def _dot_exact_tn(pieces, m):
    return _sum_dots(pieces, lambda p: lax.dot_general(p, m, TN_DIMS, preferred_element_type=F32))


def _silu(x):
    return x * jax.nn.sigmoid(x)


def _softplus(x):
    return jnp.maximum(x, 0.0) + jnp.log(1.0 + jnp.exp(-jnp.abs(x)))


def _rms(x, gain):
    return x * lax.rsqrt(jnp.mean(x * x, axis=-1, keepdims=True) + EPS) * gain


def _stream_cast(chunks, stage, sem):
    def copy(j):
        return pltpu.make_async_copy(chunks[j][0], stage.at[j % 2], sem.at[j % 2])

    copy(0).start()
    for j, (_, dst) in enumerate(chunks):
        if j + 1 < len(chunks):
            copy(j + 1).start()
        copy(j).wait()
        dst[...] = stage[j % 2].astype(BF16)


def _ffn_kernel(layer, split_tile, final_norm, with_outproj, *refs):
    refs = list(refs)
    two_in = split_tile is not None and not final_norm
    two_out = split_tile is not None and final_norm
    n_out = 2 if two_out else 1
    x_refs = [refs.pop(0) for _ in range(2 if two_in else 1)]
    mix_ref, wout_hbm = (refs.pop(0), refs.pop(0)) if with_outproj else (None, None)
    g_ref, wgu_hbm, wd_hbm = (refs.pop(0) for _ in range(3))
    fg_ref = refs.pop(0) if final_norm else None
    o_refs = [refs.pop(0) for _ in range(n_out)]
    wgu_ref, wd_ref = refs.pop(0), refs.pop(0)
    wout_ref = refs.pop(0) if with_outproj else None
    stage_cols, stage_rows, sem = refs
    i = pl.program_id(0)

    @pl.when(i == 0)
    def _():
        cols = [(wgu_hbm.at[layer, :, pl.ds(j * WEIGHT_CHUNK, WEIGHT_CHUNK)],
                 wgu_ref.at[:, pl.ds(j * WEIGHT_CHUNK, WEIGHT_CHUNK)]) for j in range(2 * D_FF // WEIGHT_CHUNK)]
        _stream_cast(cols, stage_cols, sem)
        rows = [(wd_hbm.at[layer, pl.ds(j * WEIGHT_ROWS, WEIGHT_ROWS), :],
                 wd_ref.at[pl.ds(j * WEIGHT_ROWS, WEIGHT_ROWS), :]) for j in range(D_FF // WEIGHT_ROWS)]
        if with_outproj:
            rows += [(wout_hbm.at[layer, pl.ds(j * WEIGHT_ROWS, WEIGHT_ROWS), :],
                      wout_ref.at[pl.ds(j * WEIGHT_ROWS, WEIGHT_ROWS), :]) for j in range(D_MIX // WEIGHT_ROWS)]
        _stream_cast(rows, stage_rows, sem)

    x = jnp.where(i < split_tile, x_refs[0][...], x_refs[1][...]) if two_in else x_refs[0][...]
    if with_outproj:
        x = x + jnp.dot(mix_ref[...], wout_ref[...], preferred_element_type=F32)
    xn = _rms(x, g_ref[...]).astype(BF16)
    gate = jnp.dot(xn, wgu_ref[:, :D_FF], preferred_element_type=F32)
    up = jnp.dot(xn, wgu_ref[:, D_FF:], preferred_element_type=F32)
    h = (_silu(gate) * up).astype(BF16)
    y = x + 0.5 * jnp.dot(h, wd_ref[...], preferred_element_type=F32)
    if final_norm:
        y = _rms(y, fg_ref[...])
    if two_out:
        @pl.when(i < split_tile)
        def _():
            o_refs[0][...] = y

        @pl.when(i >= split_tile)
        def _():
            o_refs[1][...] = y
    else:
        o_refs[0][...] = y


def _inproj_kernel(x_ref, g_ref, w_ref, o_ref):
    xn = _rms(x_ref[...], g_ref[...]).astype(BF16)
    o_ref[...] = jnp.dot(xn, w_ref[...], preferred_element_type=F32)


def _resident(shape):
    nd = len(shape)
    return pl.BlockSpec(shape, lambda *_: (0,) * nd, pipeline_mode=pl.Buffered(1))


def _layer_resident(arr, layer):
    zeros = (0,) * (arr.ndim - 1)
    return pl.BlockSpec((None,) + arr.shape[1:], lambda *_: (layer,) + zeros, pipeline_mode=pl.Buffered(1))


def _ffn(xs, layer, gains, wgu, wd, final_gain=None, split_rows=None, mix=None, wout=None):
    assert (2 * D_FF) % WEIGHT_CHUNK == 0 and D_FF % WEIGHT_ROWS == 0 and D_MIX % WEIGHT_ROWS == 0
    t = sum(x.shape[0] for x in xs)
    two_in = len(xs) == 2
    two_out = final_gain is not None and split_rows is not None
    split_tile = None
    if two_in or two_out:
        split_tile = (xs[0].shape[0] if two_in else split_rows) // DENSE_TILE
    tile = pl.BlockSpec((DENSE_TILE, D_MODEL), lambda i: (i, 0))
    first = pl.BlockSpec((DENSE_TILE, D_MODEL), lambda i: (jnp.minimum(i, split_tile - 1), 0))
    second = pl.BlockSpec((DENSE_TILE, D_MODEL), lambda i: (jnp.maximum(i - split_tile, 0), 0))
    in_specs = [first, second] if two_in else [tile]
    args = list(xs)
    hbm = pl.BlockSpec(memory_space=pl.ANY)
    scratch = [pltpu.VMEM((D_MODEL, 2 * D_FF), BF16), pltpu.VMEM((D_FF, D_MODEL), BF16)]
    if mix is not None:
        in_specs += [pl.BlockSpec((DENSE_TILE, D_MIX), lambda i: (i, 0)), hbm]
        args += [mix, wout]
        scratch.append(pltpu.VMEM((D_MIX, D_MODEL), BF16))
    scratch += [pltpu.VMEM((2, D_MODEL, WEIGHT_CHUNK), F32), pltpu.VMEM((2, WEIGHT_ROWS, D_MODEL), F32),
                pltpu.SemaphoreType.DMA((2,))]
    in_specs += [_layer_resident(gains, layer), hbm, hbm]
    args += [gains, wgu, wd]
    if final_gain is not None:
        in_specs.append(_resident((1, D_MODEL)))
        args.append(final_gain.reshape(1, D_MODEL))
    if two_out:
        out_specs = [first, second]
        out_shape = [jax.ShapeDtypeStruct((split_rows, D_MODEL), F32),
                     jax.ShapeDtypeStruct((t - split_rows, D_MODEL), F32)]
    else:
        out_specs, out_shape = tile, jax.ShapeDtypeStruct((t, D_MODEL), F32)
    return pl.pallas_call(
        functools.partial(_ffn_kernel, layer, split_tile, final_gain is not None, mix is not None),
        grid=(t // DENSE_TILE,), in_specs=in_specs, out_specs=out_specs, out_shape=out_shape,
        scratch_shapes=scratch,
        compiler_params=pltpu.CompilerParams(dimension_semantics=("arbitrary",),
                                             vmem_limit_bytes=VMEM_LIMIT_BYTES),
        name="ffn" if final_gain is None else "ffn_final_norm")(*args)


def _inproj(x, layer, gains, w):
    t = x.shape[0]
    return pl.pallas_call(
        _inproj_kernel, grid=(t // DENSE_TILE,),
        in_specs=[pl.BlockSpec((DENSE_TILE, D_MODEL), lambda i: (i, 0)), _layer_resident(gains, layer),
                  _layer_resident(w, layer)],
        out_specs=pl.BlockSpec((DENSE_TILE, PROJ_COLS), lambda i: (i, 0)),
        out_shape=jax.ShapeDtypeStruct((t, PROJ_COLS), F32),
        compiler_params=pltpu.CompilerParams(dimension_semantics=("parallel",), vmem_limit_bytes=VMEM_LIMIT_BYTES),
        name="inproj")(x, gains, w)


CONST_NAMES = ("tril", "negi", "blk", "merge", "sel", "lv", "expand", "bc_dn", "bc_beta", "bc_ssd")


def _mixer_constants(nseq, seq_len):
    c = nseq * seq_len
    idx = np.arange(c)
    seq, pos = idx // seq_len, idx % seq_len
    same = seq[:, None] == seq[None, :]
    pi, pj = pos[:, None], pos[None, :]
    causal = same & (pi >= pj)
    base = min(seq_len, NEUMANN_BLOCK)
    merges = []
    s = base
    while s < seq_len:
        merges.append(same & (pi // (2 * s) == pj // (2 * s)) & (pi // s != pj // s))
        s *= 2
    levels = []
    m = seq_len // 2
    while m >= 1:
        levels.append(m)
        m //= 2
    sel = np.zeros((len(levels), c, c), np.float32)
    lv = np.full((c, c), -1, np.int32)
    xor = pi ^ pj
    for k, m in enumerate(levels):
        mid = seq * seq_len + (pos // (2 * m)) * (2 * m) + (m - 1)
        sel[k, idx, mid] = 1.0
        lv[same & (pi > pj) & (xor >= m) & (xor < 2 * m)] = k
    lv[idx, idx] = len(levels)
    expand = np.zeros((LANES, M2_DIM), np.float32)
    bc_ssd = np.zeros((LANES, M2_HEADS * LANES), np.float32)
    for h in range(M2_HEADS):
        expand[SM_MDT + h, h * M2_HEADDIM:(h + 1) * M2_HEADDIM] = 1.0
        bc_ssd[SM_MDT + h, h * LANES:(h + 1) * LANES] = 1.0
    bc_dn = np.zeros((LANES, DN_HEADS * LANES), np.float32)
    bc_beta = np.zeros((LANES, DN_HEADS * LANES), np.float32)
    for h in range(DN_HEADS):
        bc_dn[SM_DA + h, h * LANES:(h + 1) * LANES] = 1.0
        bc_beta[SM_DB + h, h * LANES:(h + 1) * LANES] = 1.0
    consts = dict(
        tril=jnp.asarray(causal, BF16),
        negi=jnp.asarray(np.where(causal, 0.0, MASKED), F32),
        blk=jnp.asarray(same & (pi // base == pj // base), F32),
        merge=jnp.asarray(np.stack(merges) if merges else np.zeros((1, SUBLANES, LANES)), F32),
        sel=jnp.asarray(sel.reshape(len(levels) * c, c), BF16),
        lv=jnp.asarray(lv),
        expand=jnp.asarray(expand, BF16),
        bc_dn=jnp.asarray(bc_dn, BF16),
        bc_beta=jnp.asarray(bc_beta, BF16),
        bc_ssd=jnp.asarray(bc_ssd, BF16),
    )
    nsquare = int(math.log2(base)) - 1
    return consts, len(levels), len(merges), nsquare


def _round_robin(*staged):
    pending = list(staged)
    while pending:
        for item in list(pending):
            g, per_turn = item
            try:
                for _ in range(per_turn):
                    next(g)
            except StopIteration:
                pending.remove(item)


def _mixer_kernel(nseq, seq_len, nlev, nmerge, nsquare, single_chunk, n_alias,
                  proj_ref, conv0_ref, dn0_ref, gla0_ref, ssm0_ref,
                  convw_ref, convb_ref, alog_ref, dtb_ref, dn_norm_ref, gla_wup_ref, gla_bup_ref, gla_norm_ref,
                  m2_d_ref, m2_norm_ref,
                  tril_ref, negi_ref, blk_ref, merge_ref, sel_ref, lv_ref, expand_ref, bc_dn_ref, bc_beta_ref,
                  bc_ssd_ref, *rest):
    mix_ref, convn_ref, dnn_ref, glan_ref, ssmn_ref, cbuf, cvout = rest[n_alias:]
    c = nseq * seq_len
    chunk = pl.program_id(1)
    last_chunk = pl.num_programs(1) - 1
    seq_rows = [slice(b * seq_len, (b + 1) * seq_len) for b in range(nseq)]

    def at_first_chunk(fn):
        if single_chunk:
            fn()
        else:
            pl.when(chunk == 0)(fn)

    def at_last_chunk(fn):
        if single_chunk:
            fn()
        else:
            pl.when(chunk == last_chunk)(fn)

    @at_first_chunk
    def _():
        cbuf[:, pl.ds(CONV_PAD - CONV_TAIL, CONV_TAIL), :] = conv0_ref[...]
        if not single_chunk:
            dnn_ref[...] = dn0_ref[...]
            glan_ref[...] = gla0_ref[...]
            ssmn_ref[...] = ssm0_ref[...]

    cbuf[:, pl.ds(CONV_PAD, seq_len), :] = proj_ref[:, 0:CONV_CH].reshape(nseq, seq_len, CONV_CH)

    def conv(lo, width):
        acc = convb_ref[:, lo:lo + width].reshape(1, 1, width)
        for w in range(CONV_W):
            tap = convw_ref[w:w + 1, lo:lo + width].reshape(1, 1, width)
            acc = acc + tap * cbuf[:, pl.ds(CONV_PAD - CONV_TAIL + w, seq_len), lo:lo + width]
        return _silu(acc).reshape(c, width)

    for lo in range(0, CONV_CH, 512):
        cvout[:, lo:lo + 512] = conv(lo, 512)
    dq_all = cvout[:, OFF_DQ:OFF_DQ + DN_QK]
    dk_all = cvout[:, OFF_DK:OFF_DK + DN_QK]
    dv_all = cvout[:, OFF_DV:OFF_DV + DN_V]
    mx_all = [cvout[:, OFF_MX + p * LANES:OFF_MX + (p + 1) * LANES] for p in range(M2_PAIRS)]
    mb_all = [cvout[:, OFF_MB + g * M2_STATE:OFF_MB + (g + 1) * M2_STATE] for g in range(M2_GROUPS)]
    mc_all = [cvout[:, OFF_MC + g * M2_STATE:OFF_MC + (g + 1) * M2_STATE] for g in range(M2_GROUPS)]

    ones_sq = jnp.ones((LANES, LANES), BF16)

    def row_rsqrt(x, mult):
        return row_rsqrt_of_squares(x * x, mult)

    def row_rsqrt_of_squares(sq, mult):
        sums = jnp.dot(sq.astype(BF16), ones_sq, preferred_element_type=F32)
        return lax.rsqrt(sums * mult + EPS)

    def per_seq_last(x):
        parts = [jnp.broadcast_to(x[r.stop - 1:r.stop, :], (seq_len, x.shape[1])) for r in seq_rows]
        return parts[0] if nseq == 1 else jnp.concatenate(parts, axis=0)

    def cat_rows(parts):
        return parts[0] if nseq == 1 else jnp.concatenate(parts, axis=0)

    tril = tril_ref[...]
    negi = negi_ref[...]
    lv = lv_ref[...]
    diag = lv == nlev

    small = proj_ref[:, OFF_SMALL:OFF_SMALL + LANES]
    sp = _softplus(small + dtb_ref[...])
    g2 = (-LOG2E * jnp.exp(alog_ref[...])) * sp
    cum = _dot_exact_lhs(tril, _split3(g2))
    cum_t = cum.T
    last = per_seq_last(cum)
    e_cum = jnp.exp2(cum)
    e_rem = jnp.exp2(last - cum)
    e_last = jnp.exp2(last)
    cum_p, e_cum_p, e_rem_p = _split3(cum), _split3(e_cum), _split3(e_rem)

    def deltanet():
        bc_dn = bc_dn_ref[...]
        cum_col = _dot_exact_rhs(cum_p, bc_dn)
        egc_col = _dot_exact_rhs(e_cum_p, bc_dn)
        erem_col = _dot_exact_rhs(e_rem_p, bc_dn)
        beta_col = _dot_exact_rhs(_split3(jax.nn.sigmoid(small)), bc_beta_ref[...])
        yield
        dn = []
        for h in range(DN_HEADS):
            hs = slice(h * DN_DK, (h + 1) * DN_DK)
            dq, dk = dq_all[:, hs], dk_all[:, hs]
            st = dict(q=dq * (row_rsqrt(dq, 1.0) * (DN_DK ** -0.5)), k=dk * row_rsqrt(dk, 1.0),
                      v=dv_all[:, hs], beta=beta_col[:, hs], egc=egc_col[:, hs], erem=erem_col[:, hs])
            dn.append(st)
        yield
        for h, st in enumerate(dn):
            kb, qb = st["k"].astype(BF16), st["q"].astype(BF16)
            decay = jnp.exp2(cum_col[:, h * LANES:h * LANES + c] - cum_t[SM_DA + h:SM_DA + h + 1, :] + negi)
            kk = lax.dot_general(kb, kb, NT_DIMS, preferred_element_type=F32)
            qk = lax.dot_general(qb, kb, NT_DIMS, preferred_element_type=F32)
            st["a"] = jnp.where(diag, 0.0, kk * decay * st["beta"][:, :c])
            st["qkd"] = qk * decay
        yield
        blk = blk_ref[...]
        for st in dn:
            st["p"] = st["a"] * blk
            st["n"] = -st["p"]
        for _ in range(nsquare):
            for st in dn:
                st["p"] = _dot(st["p"], st["p"])
            yield
            for st in dn:
                st["n"] = st["n"] + st["p"] + _dot(st["n"], st["p"])
            yield
        for lvl in range(nmerge):
            emask = merge_ref[lvl]
            for st in dn:
                e = st["a"] * emask
                st["y"] = e + _dot(e, st["n"])
            yield
            for st in dn:
                st["n"] = st["n"] - st["y"] - _dot(st["n"], st["y"])
            yield
        for st in dn:
            rhs = jnp.concatenate([st["k"] * (st["beta"] * st["egc"]), st["v"] * st["beta"]], axis=1)
            st["sol"] = rhs + _dot(st["n"], rhs)
        yield
        for h, st in enumerate(dn):
            sol = st["sol"]
            w_mat, u_mat = sol[:, :DN_DK], sol[:, DN_DK:]
            qe = st["q"] * st["egc"]
            v_parts, o_parts, states = [], [], []
            for b, r in enumerate(seq_rows):
                s = dn0_ref[b, h] if single_chunk else dnn_ref[b, h]
                both = _dot(jnp.concatenate([w_mat[r], qe[r]], axis=0), s)
                v_parts.append(u_mat[r] - both[:seq_len])
                o_parts.append(both[seq_len:])
                states.append(s)
            st["v_new"], st["o"], st["s"] = cat_rows(v_parts), cat_rows(o_parts), states
        yield
        for h, st in enumerate(dn):
            v_new = st["v_new"]
            o = st["o"] + _dot(st["qkd"], v_new)
            kdec = st["k"] * st["erem"]
            lane = SM_DA + h
            for b, r in enumerate(seq_rows):
                dnn_ref[b, h] = (e_last[r.start:r.start + 1, lane:lane + 1] * st["s"][b]
                                 + _dot_tn(kdec[r], v_new[r]))
            st["o"] = o
        yield
        for h, st in enumerate(dn):
            o = st["o"]
            z = proj_ref[:, OFF_DZ + h * DN_DV:OFF_DZ + (h + 1) * DN_DV]
            o_n = o * row_rsqrt(o, 1.0 / DN_DV) * dn_norm_ref[...]
            mix_ref[:, h * DN_DV:(h + 1) * DN_DV] = (o_n * _silu(z)).astype(BF16)

    def gla():
        glow = small[:, SM_GLOW:SM_GLOW + GLA_RANK]
        zg = _dot(glow, gla_wup_ref[...]) + gla_bup_ref[...]
        gk2 = _softplus(-zg) * (-LOG2E / GLA_NORMALIZER)
        gcum = _dot_exact_lhs(tril, _split3(gk2))
        yield
        gq = proj_ref[:, OFF_GQ:OFF_GQ + GLA_QK] * (GLA_DK ** -0.5)
        gkk = proj_ref[:, OFF_GK:OFF_GK + GLA_QK]
        gq_b, gkk_b = gq.astype(BF16), gkk.astype(BF16)
        gcum_p = _split3(gcum)
        gmid_all = _dot_exact_lhs(sel_ref[...], gcum_p)
        yield
        level_ops = []
        for lev in range(nlev):
            e = jnp.exp2(-jnp.abs(gcum - gmid_all[lev * c:(lev + 1) * c, :]))
            level_ops.append(((gq * e).astype(BF16), (gkk * e).astype(BF16), lv == lev))
            if lev % 2 == 1:
                yield
        glast = per_seq_last(gcum)
        q_inter = gq * jnp.exp2(gcum)
        k_dec = gkk * jnp.exp2(glast - gcum)
        eye_dk = (lax.broadcasted_iota(jnp.int32, (GLA_DK, GLA_DK), 0)
                  == lax.broadcasted_iota(jnp.int32, (GLA_DK, GLA_DK), 1))
        ones_dk = jnp.ones((GLA_DK, GLA_DV), BF16)
        yield
        for h in range(GLA_HEADS):
            hs = slice(h * GLA_DK, (h + 1) * GLA_DK)
            v = proj_ref[:, OFF_GV + h * GLA_DV:OFF_GV + (h + 1) * GLA_DV]
            scores = jnp.where(diag, lax.dot_general(gq_b[:, hs], gkk_b[:, hs], NT_DIMS,
                                                     preferred_element_type=F32), 0.0)
            for q_t, k_t, mask in level_ops:
                s_lev = lax.dot_general(q_t[:, hs], k_t[:, hs], NT_DIMS, preferred_element_type=F32)
                scores = jnp.where(mask, s_lev, scores)
            yield
            o_parts = []
            for b, r in enumerate(seq_rows):
                s = gla0_ref[b, h] if single_chunk else glan_ref[b, h]
                o_parts.append(_dot(q_inter[r, hs], s))
                lam = jnp.where(eye_dk, jnp.exp2(glast[r.start:r.start + 1, hs]), 0.0)
                keep = _dot_exact_rhs(_split3(lam), ones_dk)
                glan_ref[b, h] = keep * s + _dot_tn(k_dec[r, hs], v[r])
            o = cat_rows(o_parts) + _dot(scores, v)
            yield
            z = proj_ref[:, OFF_GG + h * GLA_DV:OFF_GG + (h + 1) * GLA_DV]
            o_n = o * row_rsqrt(o, 1.0 / GLA_DV) * gla_norm_ref[...]
            mix_ref[:, DN_V + h * GLA_DV:DN_V + (h + 1) * GLA_DV] = (o_n * _silu(z)).astype(BF16)

    def ssd():
        expand = expand_ref[...]
        sp_p = _split3(sp)
        dt_x = _dot_exact_rhs(sp_p, expand)
        ecum_x = _dot_exact_rhs(e_cum_p, expand)
        dtrem_x = _dot_exact_rhs(_split3(sp * e_rem), expand)
        yield
        cum_cols = _dot_exact_rhs(cum_p, bc_ssd_ref[...])
        lane_lo = lax.broadcasted_iota(jnp.int32, (c, LANES), 1) < M2_HEADDIM
        row_lo = lax.broadcasted_iota(jnp.int32, (LANES, LANES), 0) < M2_HEADDIM
        yield
        m2_off = DN_V + GLA_V
        total = jnp.zeros((c, LANES), F32)
        gated = []
        for g in range(M2_GROUPS):
            bm, cm = mb_all[g], mc_all[g]
            bm_b, cm_b = bm.astype(BF16), cm.astype(BF16)
            cb = lax.dot_general(cm_b, bm_b, NT_DIMS, preferred_element_type=F32)
            yield
            for pp in range(M2_PAIRS_PER_GROUP):
                p = g * M2_PAIRS_PER_GROUP + pp
                ps = slice(p * LANES, (p + 1) * LANES)
                ha, hb = SM_MDT + 2 * p, SM_MDT + 2 * p + 1
                xh = mx_all[p]
                x_dt = xh * dt_x[:, ps]
                col_a, col_b = 2 * p * LANES, (2 * p + 1) * LANES
                dec_a = jnp.exp2(cum_cols[:, col_a:col_a + c] - cum_t[ha:ha + 1, :] + negi)
                dec_b = jnp.exp2(cum_cols[:, col_b:col_b + c] - cum_t[hb:hb + 1, :] + negi)
                m_pair = jnp.concatenate([cb * dec_a, cb * dec_b], axis=1)
                x_bd = jnp.concatenate([jnp.where(lane_lo, x_dt, 0.0), jnp.where(lane_lo, 0.0, x_dt)], axis=0)
                x_rem = (xh * dtrem_x[:, ps]).astype(BF16)
                inter = []
                for b, r in enumerate(seq_rows):
                    s = ssm0_ref[b, p] if single_chunk else ssmn_ref[b, p]
                    inter.append(lax.dot_general(cm_b[r], s.astype(BF16), NT_DIMS, preferred_element_type=F32))
                    keep = jnp.where(row_lo, e_last[r.start:r.start + 1, ha:ha + 1],
                                     e_last[r.start:r.start + 1, hb:hb + 1])
                    ssmn_ref[b, p] = keep * s + lax.dot_general(x_rem[r], bm_b[r], TN_DIMS,
                                                                preferred_element_type=F32)
                y = _dot(m_pair, x_bd) + cat_rows(inter) * ecum_x[:, ps] + m2_d_ref[:, ps] * xh
                z = proj_ref[:, OFF_MZ + p * LANES:OFF_MZ + (p + 1) * LANES]
                yz = y * _silu(z)
                total = total + yz * yz
                gated.append(yz)
                yield
        scale = row_rsqrt_of_squares(total, 1.0 / M2_DIM)
        for p, yz in enumerate(gated):
            ps = slice(p * LANES, (p + 1) * LANES)
            mix_ref[:, m2_off + p * LANES:m2_off + (p + 1) * LANES] = (yz * scale * m2_norm_ref[:, ps]).astype(BF16)

    _round_robin((ssd(), 1), (gla(), 1), (deltanet(), 1))

    @at_last_chunk
    def _():
        convn_ref[...] = cbuf[:, pl.ds(CONV_PAD + seq_len - CONV_TAIL, CONV_TAIL), :]

    if not single_chunk:
        cbuf[:, pl.ds(0, CONV_PAD), :] = cbuf[:, pl.ds(seq_len, CONV_PAD), :]


def _mixer(proj, row_start, nb, seq, nseq, seq_len, layer, conv0, dn0, gla0, ssm0, params,
           total_rows, prev_mix=None, prev_states=None):
    c = nseq * seq_len
    nc = seq // seq_len
    blk0 = row_start // c
    lsel = layer if conv0.shape[0] > 1 else 0
    consts, nlev, nmerge, nsquare = _mixer_constants(nseq, seq_len)
    const_args = [consts[k] for k in CONST_NAMES]

    def state_spec(shape):
        zeros = (0,) * len(shape)
        return pl.BlockSpec((None, nseq) + shape, lambda b, k: (lsel, b) + zeros)

    def out_state_spec(shape):
        zeros = (0,) * len(shape)
        return pl.BlockSpec((None, nseq) + shape, lambda b, k: (layer, b) + zeros)

    conv_shape = (CONV_TAIL, CONV_CH)
    dn_shape = (DN_HEADS, DN_DK, DN_DV)
    gla_shape = (GLA_HEADS, GLA_DK, GLA_DV)
    ssm_shape = (M2_PAIRS, 2 * M2_HEADDIM, M2_STATE)
    state_shapes = (conv_shape, dn_shape, gla_shape, ssm_shape)
    in_specs = [pl.BlockSpec((c, PROJ_COLS), lambda b, k: (blk0 + b * nc + k, 0))]
    in_specs += [state_spec(s) for s in state_shapes]
    in_specs += [_resident(p.shape) for p in params]
    in_specs += [_resident(a.shape) for a in const_args]
    args = [proj, conv0, dn0, gla0, ssm0, *params, *const_args]
    aliases = {}
    if prev_mix is not None:
        aliases[len(args)] = 0
        args.append(prev_mix)
    if prev_states is not None:
        for i, st in enumerate(prev_states):
            aliases[len(args)] = 1 + i
            args.append(st)
    n_alias = len(aliases)
    in_specs += [pl.BlockSpec(memory_space=pl.ANY)] * n_alias
    out_specs = [pl.BlockSpec((c, D_MIX), lambda b, k: (blk0 + b * nc + k, 0))]
    out_specs += [out_state_spec(s) for s in state_shapes]
    out_shape = [jax.ShapeDtypeStruct((total_rows, D_MIX), BF16)]
    out_shape += [jax.ShapeDtypeStruct((DEPTH, nb) + s, F32) for s in state_shapes]
    body = functools.partial(_mixer_kernel, nseq, seq_len, nlev, nmerge, nsquare, nc == 1, n_alias)
    return pl.pallas_call(
        body, grid=(nb // nseq, nc), in_specs=in_specs, out_specs=out_specs, out_shape=out_shape,
        input_output_aliases=aliases,
        scratch_shapes=[pltpu.VMEM((nseq, CONV_PAD + seq_len, CONV_CH), F32), pltpu.VMEM((c, CONV_CH), F32)],
        compiler_params=pltpu.CompilerParams(dimension_semantics=("parallel", "arbitrary"),
                                             vmem_limit_bytes=VMEM_LIMIT_BYTES),
        name=f"mixer_{nseq}x{seq_len}")(*args)


def _permute_w_in(w):
    o = CONV_CH
    dz = w[..., o:o + DN_V]; o += DN_V
    da = w[..., o:o + DN_HEADS]; o += DN_HEADS
    db = w[..., o:o + DN_HEADS]; o += DN_HEADS
    gq = w[..., o:o + GLA_QK]; o += GLA_QK
    gk = w[..., o:o + GLA_QK]; o += GLA_QK
    gv = w[..., o:o + GLA_V]; o += GLA_V
    glow = w[..., o:o + GLA_RANK]; o += GLA_RANK
    gg = w[..., o:o + GLA_V]; o += GLA_V
    mz = w[..., o:o + M2_DIM]; o += M2_DIM
    mdt = w[..., o:o + M2_HEADS]; o += M2_HEADS
    assert o == w.shape[-1]
    pad = jnp.zeros(w.shape[:-1] + (LANES - SM_USED,), w.dtype)
    return jnp.concatenate([w[..., :CONV_CH], dz, gq, gk, gv, gg, mz, da, mdt, db, glow, pad], axis=-1)


def _small_lane_row(dn_vals, m2_vals):
    pad = jnp.zeros((LANES - SM_DB,), F32)
    return jnp.concatenate([dn_vals.astype(F32), m2_vals.astype(F32), pad]).reshape(1, LANES)


def _ssm_to_pairs(s):
    t = jnp.swapaxes(s, -1, -2)
    return t.reshape(t.shape[:-3] + (M2_PAIRS, 2 * M2_HEADDIM, M2_STATE))


def _ssm_from_pairs(s):
    t = s.reshape(s.shape[:-3] + (M2_HEADS, M2_HEADDIM, M2_STATE))
    return jnp.swapaxes(t, -1, -2)


def kernel(x_prompt, x_sample, state_conv, state_delta, state_gla, state_ssm, ffn1_norm, ffn1_w_gu, ffn1_w_down, mix_norm, w_in, conv_w, conv_b, dn_a_log, dn_dt_bias, dn_norm, gla_w_up, gla_b_up, gla_norm, m2_a_log, m2_dt_bias, m2_d, m2_norm, w_out, ffn2_norm, ffn2_w_gu, ffn2_w_down, final_norm):
    bp, sp, _ = x_prompt.shape
    bs, ss, _ = x_sample.shape
    tp, ts = bp * sp, bs * ss
    assert tp % DENSE_TILE == 0 and ts % DENSE_TILE == 0 and sp % PROMPT_CHUNK == 0
    assert bs % SAMPLE_SEQS_PER_STEP == 0 and tp % (SAMPLE_SEQS_PER_STEP * ss) == 0 and ss % SUBLANES == 0

    zero_conv = jnp.zeros((1, bp, CONV_TAIL, CONV_CH), F32)
    zero_dn = jnp.zeros((1, bp, DN_HEADS, DN_DK, DN_DV), F32)
    zero_gla = jnp.zeros((1, bp, GLA_HEADS, GLA_DK, GLA_DV), F32)
    zero_ssm = jnp.zeros((1, bp, M2_PAIRS, 2 * M2_HEADDIM, M2_STATE), F32)
    ssm_pairs = _ssm_to_pairs(state_ssm)

    wgu1, wd1, wgu2, wd2, wout = ffn1_w_gu, ffn1_w_down, ffn2_w_gu, ffn2_w_down, w_out
    win = _permute_w_in(w_in.astype(BF16))
    g_ffn1, g_mix, g_ffn2 = (g.reshape(DEPTH, 1, D_MODEL) for g in (ffn1_norm, mix_norm, ffn2_norm))

    xs = [x_prompt.reshape(tp, D_MODEL), x_sample.reshape(ts, D_MODEL)]
    st_p = st_s = None
    for l in range(DEPTH):
        params = (conv_w[l], conv_b[l].reshape(1, CONV_CH),
                  _small_lane_row(dn_a_log[l], m2_a_log[l]), _small_lane_row(dn_dt_bias[l], m2_dt_bias[l]),
                  dn_norm[l].reshape(1, DN_DV),
                  gla_w_up[l].astype(BF16), gla_b_up[l].reshape(1, GLA_QK), gla_norm[l].reshape(1, GLA_DV),
                  jnp.repeat(m2_d[l], M2_HEADDIM).reshape(1, M2_DIM), m2_norm[l].reshape(1, M2_DIM))

        x = _ffn(xs, l, g_ffn1, wgu1, wd1)
        proj = _inproj(x, l, g_mix, win)
        mix, *st_p = _mixer(proj, 0, bp, sp, 1, PROMPT_CHUNK, l, zero_conv, zero_dn, zero_gla, zero_ssm, params,
                            tp + ts, prev_states=st_p)
        mix, *st_s = _mixer(proj, tp, bs, ss, SAMPLE_SEQS_PER_STEP, ss, l,
                            state_conv, state_delta, state_gla, ssm_pairs, params,
                            tp + ts, prev_mix=mix, prev_states=st_s)
        if l < DEPTH - 1:
            xs = [_ffn([x], l, g_ffn2, wgu2, wd2, mix=mix, wout=wout)]
        else:
            y_prompt, y_sample = _ffn([x], l, g_ffn2, wgu2, wd2, final_gain=final_norm, split_rows=tp,
                                      mix=mix, wout=wout)

    return (y_prompt.reshape(bp, sp, D_MODEL), y_sample.reshape(bs, ss, D_MODEL),
            st_p[0], st_p[1], st_p[2], _ssm_from_pairs(st_p[3]),
            st_s[0], st_s[1], st_s[2], _ssm_from_pairs(st_s[3]))
```

```python
import functools
import math

import jax
import jax.numpy as jnp
import numpy as np
from jax import lax
from jax.experimental import pallas as pl
from jax.experimental.pallas import tpu as pltpu

F32 = jnp.float32
BF16 = jnp.bfloat16
LOG2E = math.log2(math.e)

D_MODEL = 1024
DEPTH = 2
D_MIX = 2 * D_MODEL
D_FF = 2816
CONV_W = 4
DN_HEADS = 4
DN_DK = 128
DN_DV = 128
GLA_HEADS = 4
GLA_DK = 64
GLA_DV = 128
GLA_RANK = 16
GLA_NORMALIZER = 16.0
M2_DIM = 1024
M2_HEADDIM = 64
M2_HEADS = 16
M2_STATE = 128
M2_GROUPS = 2
M2_PAIRS = M2_HEADS // 2
M2_PAIRS_PER_GROUP = M2_PAIRS // M2_GROUPS
EPS = 1e-6
LANES = 128
SUBLANES = 8

DN_QK = DN_HEADS * DN_DK
DN_V = DN_HEADS * DN_DV
GLA_QK = GLA_HEADS * GLA_DK
GLA_V = GLA_HEADS * GLA_DV
M2_BC = M2_GROUPS * M2_STATE
CONV_CH = 3 * 512 + M2_DIM + 2 * M2_BC

OFF_DQ = 0
OFF_DK = OFF_DQ + DN_QK
OFF_DV = OFF_DK + DN_QK
OFF_MX = OFF_DV + DN_V
OFF_MB = OFF_MX + M2_DIM
OFF_MC = OFF_MB + M2_BC
OFF_DZ = CONV_CH
OFF_GQ = OFF_DZ + DN_V
OFF_GK = OFF_GQ + GLA_QK
OFF_GV = OFF_GK + GLA_QK
OFF_GG = OFF_GV + GLA_V
OFF_MZ = OFF_GG + GLA_V
OFF_SMALL = OFF_MZ + M2_DIM
SM_DA = 0
SM_MDT = SM_DA + DN_HEADS
SM_DB = SM_MDT + M2_HEADS
SM_GLOW = SM_DB + DN_HEADS
SM_USED = SM_GLOW + GLA_RANK
PROJ_COLS = OFF_SMALL + LANES

CONV_PAD = SUBLANES
CONV_TAIL = CONV_W - 1
NEUMANN_BLOCK = 16
VMEM_LIMIT_BYTES = 56 * 1024 * 1024
PROMPT_CHUNK = 128
SAMPLE_SEQS_PER_STEP = 8
DENSE_TILE = 256
WEIGHT_CHUNK = 512
WEIGHT_ROWS = 256
TRANSPOSE_ROWS = 896
MASKED = -1e30
NT_DIMS = (((1,), (1,)), ((), ()))
TN_DIMS = (((0,), (0,)), ((), ()))


def _dot(a, b):
    return jnp.dot(a.astype(BF16), b.astype(BF16), preferred_element_type=F32)


def _dot_nt(a, b):
    return lax.dot_general(a.astype(BF16), b.astype(BF16), NT_DIMS, preferred_element_type=F32)


def _dot_tn(a, b):
    return lax.dot_general(a.astype(BF16), b.astype(BF16), TN_DIMS, preferred_element_type=F32)


def _split2(x):
    hi = x.astype(BF16)
    return hi, (x - hi.astype(F32)).astype(BF16)


def _split3(x):
    hi = x.astype(BF16)
    r1 = x - hi.astype(F32)
    mid = r1.astype(BF16)
    lo = (r1 - mid.astype(F32)).astype(BF16)
    return hi, mid, lo


def _sum_dots(pieces, fn):
    acc = None
    for piece in pieces:
        part = fn(piece)
        acc = part if acc is None else acc + part
    return acc


def _dot_exact_lhs(m, pieces):
    return _sum_dots(pieces, lambda p: jnp.dot(m, p, preferred_element_type=F32))


def _dot_exact_rhs(pieces, m):
    return _sum_dots(pieces, lambda p: jnp.dot(p, m, preferred_element_type=F32))


def _dot_exact_tn(pieces, m):
    return _sum_dots(pieces, lambda p: lax.dot_general(p, m, TN_DIMS, preferred_element_type=F32))


def _silu(x):
    return x * jax.nn.sigmoid(x)


def _softplus(x):
    return jnp.maximum(x, 0.0) + jnp.log(1.0 + jnp.exp(-jnp.abs(x)))


def _rms(x, gain):
    return x * lax.rsqrt(jnp.mean(x * x, axis=-1, keepdims=True) + EPS) * gain


def _stream_cast(chunks, stage, sem):
    def copy(j):
        return pltpu.make_async_copy(chunks[j][0], stage.at[j % 2], sem.at[j % 2])

    copy(0).start()
    for j, (_, dst) in enumerate(chunks):
        if j + 1 < len(chunks):
            copy(j + 1).start()
        copy(j).wait()
        dst[...] = stage[j % 2].astype(BF16)


def _ffn_kernel(layer, split_tile, final_norm, with_outproj, *refs):
    refs = list(refs)
    two_in = split_tile is not None and not final_norm
    two_out = split_tile is not None and final_norm
    n_out = 2 if two_out else 1
    x_refs = [refs.pop(0) for _ in range(2 if two_in else 1)]
    mix_ref, wout_hbm = (refs.pop(0), refs.pop(0)) if with_outproj else (None, None)
    g_ref, wgu_hbm, wd_hbm = (refs.pop(0) for _ in range(3))
    fg_ref = refs.pop(0) if final_norm else None
    o_refs = [refs.pop(0) for _ in range(n_out)]
    wgu_ref, wd_ref = refs.pop(0), refs.pop(0)
    wout_ref = refs.pop(0) if with_outproj else None
    stage_cols, stage_rows, sem = refs
    i = pl.program_id(0)

    @pl.when(i == 0)
    def _():
        cols = [(wgu_hbm.at[layer, :, pl.ds(j * WEIGHT_CHUNK, WEIGHT_CHUNK)],
                 wgu_ref.at[:, pl.ds(j * WEIGHT_CHUNK, WEIGHT_CHUNK)]) for j in range(2 * D_FF // WEIGHT_CHUNK)]
        _stream_cast(cols, stage_cols, sem)
        rows = [(wd_hbm.at[layer, pl.ds(j * WEIGHT_ROWS, WEIGHT_ROWS), :],
                 wd_ref.at[pl.ds(j * WEIGHT_ROWS, WEIGHT_ROWS), :]) for j in range(D_FF // WEIGHT_ROWS)]
        if with_outproj:
            rows += [(wout_hbm.at[layer, pl.ds(j * WEIGHT_ROWS, WEIGHT_ROWS), :],
                      wout_ref.at[pl.ds(j * WEIGHT_ROWS, WEIGHT_ROWS), :]) for j in range(D_MIX // WEIGHT_ROWS)]
        _stream_cast(rows, stage_rows, sem)

    x = jnp.where(i < split_tile, x_refs[0][...], x_refs[1][...]) if two_in else x_refs[0][...]
    if with_outproj:
        x = x + jnp.dot(mix_ref[...], wout_ref[...], preferred_element_type=F32)
    xn = _rms(x, g_ref[...]).astype(BF16)
    gate = jnp.dot(xn, wgu_ref[:, :D_FF], preferred_element_type=F32)
    up = jnp.dot(xn, wgu_ref[:, D_FF:], preferred_element_type=F32)
    h = (_silu(gate) * up).astype(BF16)
    y = x + 0.5 * jnp.dot(h, wd_ref[...], preferred_element_type=F32)
    if final_norm:
        y = _rms(y, fg_ref[...])
    if two_out:
        @pl.when(i < split_tile)
        def _():
            o_refs[0][...] = y

        @pl.when(i >= split_tile)
        def _():
            o_refs[1][...] = y
    else:
        o_refs[0][...] = y


def _inproj_kernel(x_ref, g_ref, wt_ref, o_ref, w_ref):
    @pl.when(pl.program_id(0) == 0)
    def _():
        for r in range(0, PROJ_COLS, TRANSPOSE_ROWS):
            w_ref[:, r:r + TRANSPOSE_ROWS] = wt_ref[r:r + TRANSPOSE_ROWS, :].T

    xn = _rms(x_ref[...], g_ref[...]).astype(BF16)
    o_ref[...] = jnp.dot(xn, w_ref[...], preferred_element_type=F32)


def _resident(shape):
    nd = len(shape)
    return pl.BlockSpec(shape, lambda *_: (0,) * nd, pipeline_mode=pl.Buffered(1))


def _layer_resident(arr, layer):
    zeros = (0,) * (arr.ndim - 1)
    return pl.BlockSpec((None,) + arr.shape[1:], lambda *_: (layer,) + zeros, pipeline_mode=pl.Buffered(1))


def _ffn(xs, layer, gains, wgu, wd, final_gain=None, split_rows=None, mix=None, wout=None):
    assert (2 * D_FF) % WEIGHT_CHUNK == 0 and D_FF % WEIGHT_ROWS == 0 and D_MIX % WEIGHT_ROWS == 0
    t = sum(x.shape[0] for x in xs)
    two_in = len(xs) == 2
    two_out = final_gain is not None and split_rows is not None
    split_tile = None
    if two_in or two_out:
        split_tile = (xs[0].shape[0] if two_in else split_rows) // DENSE_TILE
    tile = pl.BlockSpec((DENSE_TILE, D_MODEL), lambda i: (i, 0))
    first = pl.BlockSpec((DENSE_TILE, D_MODEL), lambda i: (jnp.minimum(i, split_tile - 1), 0))
    second = pl.BlockSpec((DENSE_TILE, D_MODEL), lambda i: (jnp.maximum(i - split_tile, 0), 0))
    in_specs = [first, second] if two_in else [tile]
    args = list(xs)
    hbm = pl.BlockSpec(memory_space=pl.ANY)
    scratch = [pltpu.VMEM((D_MODEL, 2 * D_FF), BF16), pltpu.VMEM((D_FF, D_MODEL), BF16)]
    if mix is not None:
        in_specs += [pl.BlockSpec((DENSE_TILE, D_MIX), lambda i: (i, 0)), hbm]
        args += [mix, wout]
        scratch.append(pltpu.VMEM((D_MIX, D_MODEL), BF16))
    scratch += [pltpu.VMEM((2, D_MODEL, WEIGHT_CHUNK), F32), pltpu.VMEM((2, WEIGHT_ROWS, D_MODEL), F32),
                pltpu.SemaphoreType.DMA((2,))]
    in_specs += [_layer_resident(gains, layer), hbm, hbm]
    args += [gains, wgu, wd]
    if final_gain is not None:
        in_specs.append(_resident((1, D_MODEL)))
        args.append(final_gain.reshape(1, D_MODEL))
    if two_out:
        out_specs = [first, second]
        out_shape = [jax.ShapeDtypeStruct((split_rows, D_MODEL), F32),
                     jax.ShapeDtypeStruct((t - split_rows, D_MODEL), F32)]
    else:
        out_specs, out_shape = tile, jax.ShapeDtypeStruct((t, D_MODEL), F32)
    return pl.pallas_call(
        functools.partial(_ffn_kernel, layer, split_tile, final_gain is not None, mix is not None),
        grid=(t // DENSE_TILE,), in_specs=in_specs, out_specs=out_specs, out_shape=out_shape,
        scratch_shapes=scratch,
        compiler_params=pltpu.CompilerParams(dimension_semantics=("arbitrary",),
                                             vmem_limit_bytes=VMEM_LIMIT_BYTES),
        name="ffn" if final_gain is None else "ffn_final_norm")(*args)


def _inproj(x, layer, gains, wt):
    t = x.shape[0]
    assert PROJ_COLS % TRANSPOSE_ROWS == 0
    return pl.pallas_call(
        _inproj_kernel, grid=(t // DENSE_TILE,),
        in_specs=[pl.BlockSpec((DENSE_TILE, D_MODEL), lambda i: (i, 0)), _layer_resident(gains, layer),
                  _layer_resident(wt, layer)],
        out_specs=pl.BlockSpec((DENSE_TILE, PROJ_COLS), lambda i: (i, 0)),
        out_shape=jax.ShapeDtypeStruct((t, PROJ_COLS), F32),
        scratch_shapes=[pltpu.VMEM((D_MODEL, PROJ_COLS), BF16)],
        compiler_params=pltpu.CompilerParams(dimension_semantics=("arbitrary",), vmem_limit_bytes=VMEM_LIMIT_BYTES),
        name="inproj")(x, gains, wt)


CONST_NAMES = ("tril", "negi", "blk", "merge", "sel", "lv", "expand", "bc_dn", "bc_beta", "bc_ssd")


def _mixer_constants(nseq, seq_len):
    c = nseq * seq_len
    idx = np.arange(c)
    seq, pos = idx // seq_len, idx % seq_len
    same = seq[:, None] == seq[None, :]
    pi, pj = pos[:, None], pos[None, :]
    causal = same & (pi >= pj)
    base = min(seq_len, NEUMANN_BLOCK)
    merges = []
    s = base
    while s < seq_len:
        merges.append(same & (pi // (2 * s) == pj // (2 * s)) & (pi // s != pj // s))
        s *= 2
    levels = []
    m = seq_len // 2
    while m >= 1:
        levels.append(m)
        m //= 2
    sel = np.zeros((len(levels), c, c), np.float32)
    lv = np.full((c, c), -1, np.int32)
    xor = pi ^ pj
    for k, m in enumerate(levels):
        mid = seq * seq_len + (pos // (2 * m)) * (2 * m) + (m - 1)
        sel[k, idx, mid] = 1.0
        lv[same & (pi > pj) & (xor >= m) & (xor < 2 * m)] = k
    lv[idx, idx] = len(levels)
    expand = np.zeros((LANES, M2_DIM), np.float32)
    bc_ssd = np.zeros((LANES, M2_HEADS * LANES), np.float32)
    for h in range(M2_HEADS):
        expand[SM_MDT + h, h * M2_HEADDIM:(h + 1) * M2_HEADDIM] = 1.0
        bc_ssd[SM_MDT + h, h * LANES:(h + 1) * LANES] = 1.0
    bc_dn = np.zeros((LANES, DN_HEADS * LANES), np.float32)
    bc_beta = np.zeros((LANES, DN_HEADS * LANES), np.float32)
    for h in range(DN_HEADS):
        bc_dn[SM_DA + h, h * LANES:(h + 1) * LANES] = 1.0
        bc_beta[SM_DB + h, h * LANES:(h + 1) * LANES] = 1.0
    consts = dict(
        tril=jnp.asarray(causal, BF16),
        negi=jnp.asarray(np.where(causal, 0.0, MASKED), F32),
        blk=jnp.asarray(same & (pi // base == pj // base), F32),
        merge=jnp.asarray(np.stack(merges) if merges else np.zeros((1, SUBLANES, LANES)), F32),
        sel=jnp.asarray(sel.reshape(len(levels) * c, c), BF16),
        lv=jnp.asarray(lv),
        expand=jnp.asarray(expand, BF16),
        bc_dn=jnp.asarray(bc_dn, BF16),
        bc_beta=jnp.asarray(bc_beta, BF16),
        bc_ssd=jnp.asarray(bc_ssd, BF16),
    )
    nsquare = int(math.log2(base)) - 1
    return consts, len(levels), len(merges), nsquare


def _round_robin(*staged):
    pending = list(staged)
    while pending:
        for item in list(pending):
            g, per_turn = item
            try:
                for _ in range(per_turn):
                    next(g)
            except StopIteration:
                pending.remove(item)


def _mixer_kernel(nseq, seq_len, nlev, nmerge, nsquare, single_chunk, n_alias,
                  proj_ref, conv0_ref, dn0_ref, gla0_ref, ssm0_ref,
                  convw_ref, convb_ref, alog_ref, dtb_ref, dn_norm_ref, gla_wup_ref, gla_bup_ref, gla_norm_ref,
                  m2_d_ref, m2_norm_ref,
                  tril_ref, negi_ref, blk_ref, merge_ref, sel_ref, lv_ref, expand_ref, bc_dn_ref, bc_beta_ref,
                  bc_ssd_ref, *rest):
    mix_ref, convn_ref, dnn_ref, glan_ref, ssmn_ref, cbuf, cvout = rest[n_alias:]
    c = nseq * seq_len
    chunk = pl.program_id(1)
    last_chunk = pl.num_programs(1) - 1
    seq_rows = [slice(b * seq_len, (b + 1) * seq_len) for b in range(nseq)]

    def at_first_chunk(fn):
        if single_chunk:
            fn()
        else:
            pl.when(chunk == 0)(fn)

    def at_last_chunk(fn):
        if single_chunk:
            fn()
        else:
            pl.when(chunk == last_chunk)(fn)

    @at_first_chunk
    def _():
        cbuf[:, pl.ds(CONV_PAD - CONV_TAIL, CONV_TAIL), :] = conv0_ref[...]
        if not single_chunk:
            dnn_ref[...] = dn0_ref[...]
            glan_ref[...] = gla0_ref[...]
            ssmn_ref[...] = ssm0_ref[...]

    cbuf[:, pl.ds(CONV_PAD, seq_len), :] = proj_ref[:, 0:CONV_CH].reshape(nseq, seq_len, CONV_CH)

    def conv(lo, width):
        acc = convb_ref[:, lo:lo + width].reshape(1, 1, width)
        for w in range(CONV_W):
            tap = convw_ref[w:w + 1, lo:lo + width].reshape(1, 1, width)
            acc = acc + tap * cbuf[:, pl.ds(CONV_PAD - CONV_TAIL + w, seq_len), lo:lo + width]
        return _silu(acc).reshape(c, width)

    for lo in range(0, CONV_CH, 512):
        cvout[:, lo:lo + 512] = conv(lo, 512)
    dq_all = cvout[:, OFF_DQ:OFF_DQ + DN_QK]
    dk_all = cvout[:, OFF_DK:OFF_DK + DN_QK]
    dv_all = cvout[:, OFF_DV:OFF_DV + DN_V]
    mx_all = [cvout[:, OFF_MX + p * LANES:OFF_MX + (p + 1) * LANES] for p in range(M2_PAIRS)]
    mb_all = [cvout[:, OFF_MB + g * M2_STATE:OFF_MB + (g + 1) * M2_STATE] for g in range(M2_GROUPS)]
    mc_all = [cvout[:, OFF_MC + g * M2_STATE:OFF_MC + (g + 1) * M2_STATE] for g in range(M2_GROUPS)]

    ones_sq = jnp.ones((LANES, LANES), BF16)

    def row_rsqrt(x, mult):
        return row_rsqrt_of_squares(x * x, mult)

    def row_rsqrt_of_squares(sq, mult):
        sums = jnp.dot(sq.astype(BF16), ones_sq, preferred_element_type=F32)
        return lax.rsqrt(sums * mult + EPS)

    def per_seq_last(x):
        parts = [jnp.broadcast_to(x[r.stop - 1:r.stop, :], (seq_len, x.shape[1])) for r in seq_rows]
        return parts[0] if nseq == 1 else jnp.concatenate(parts, axis=0)

    def cat_rows(parts):
        return parts[0] if nseq == 1 else jnp.concatenate(parts, axis=0)

    tril = tril_ref[...]
    negi = negi_ref[...]
    lv = lv_ref[...]
    diag = lv == nlev

    small = proj_ref[:, OFF_SMALL:OFF_SMALL + LANES]
    sp = _softplus(small + dtb_ref[...])
    g2 = (-LOG2E * jnp.exp(alog_ref[...])) * sp
    cum = _dot_exact_lhs(tril, _split3(g2))
    cum_t = cum.T
    last = per_seq_last(cum)
    e_cum = jnp.exp2(cum)
    e_rem = jnp.exp2(last - cum)
    e_last = jnp.exp2(last)
    cum_p, e_cum_p, e_rem_p = _split3(cum), _split2(e_cum), _split2(e_rem)

    def deltanet():
        bc_dn = bc_dn_ref[...]
        cum_col = _dot_exact_rhs(cum_p, bc_dn)
        egc_col = _dot_exact_rhs(e_cum_p, bc_dn)
        erem_col = _dot_exact_rhs(e_rem_p, bc_dn)
        beta_col = _dot_exact_rhs(_split2(jax.nn.sigmoid(small)), bc_beta_ref[...])
        yield
        dn = []
        for h in range(DN_HEADS):
            hs = slice(h * DN_DK, (h + 1) * DN_DK)
            dq, dk = dq_all[:, hs], dk_all[:, hs]
            st = dict(q=dq * (row_rsqrt(dq, 1.0) * (DN_DK ** -0.5)), k=dk * row_rsqrt(dk, 1.0),
                      v=dv_all[:, hs], beta=beta_col[:, hs], egc=egc_col[:, hs], erem=erem_col[:, hs])
            dn.append(st)
        yield
        for h, st in enumerate(dn):
            kb, qb = st["k"].astype(BF16), st["q"].astype(BF16)
            decay = jnp.exp2(cum_col[:, h * LANES:h * LANES + c] - cum_t[SM_DA + h:SM_DA + h + 1, :] + negi)
            kk = lax.dot_general(kb, kb, NT_DIMS, preferred_element_type=F32)
            qk = lax.dot_general(qb, kb, NT_DIMS, preferred_element_type=F32)
            st["a"] = jnp.where(diag, 0.0, kk * decay * st["beta"][:, :c])
            st["qkd"] = qk * decay
        yield
        blk = blk_ref[...]
        for st in dn:
            st["p"] = st["a"] * blk
            st["n"] = -st["p"]
        for _ in range(nsquare):
            for st in dn:
                st["p"] = _dot(st["p"], st["p"])
            yield
            for st in dn:
                st["n"] = st["n"] + st["p"] + _dot(st["n"], st["p"])
            yield
        for lvl in range(nmerge):
            emask = merge_ref[lvl]
            for st in dn:
                e = st["a"] * emask
                st["y"] = e + _dot(e, st["n"])
            yield
            for st in dn:
                st["n"] = st["n"] - st["y"] - _dot(st["n"], st["y"])
            yield
        for st in dn:
            rhs = jnp.concatenate([st["k"] * (st["beta"] * st["egc"]), st["v"] * st["beta"]], axis=1)
            st["sol"] = rhs + _dot(st["n"], rhs)
        yield
        for h, st in enumerate(dn):
            sol = st["sol"]
            w_mat, u_mat = sol[:, :DN_DK], sol[:, DN_DK:]
            qe = st["q"] * st["egc"]
            v_parts, o_parts, states = [], [], []
            for b, r in enumerate(seq_rows):
                s = dn0_ref[b, h] if single_chunk else dnn_ref[b, h]
                both = _dot(jnp.concatenate([w_mat[r], qe[r]], axis=0), s)
                v_parts.append(u_mat[r] - both[:seq_len])
                o_parts.append(both[seq_len:])
                states.append(s)
            st["v_new"], st["o"], st["s"] = cat_rows(v_parts), cat_rows(o_parts), states
        yield
        for h, st in enumerate(dn):
            v_new = st["v_new"]
            o = st["o"] + _dot(st["qkd"], v_new)
            kdec = st["k"] * st["erem"]
            lane = SM_DA + h
            for b, r in enumerate(seq_rows):
                dnn_ref[b, h] = (e_last[r.start:r.start + 1, lane:lane + 1] * st["s"][b]
                                 + _dot_tn(kdec[r], v_new[r]))
            st["o"] = o
        yield
        for h, st in enumerate(dn):
            o = st["o"]
            z = proj_ref[:, OFF_DZ + h * DN_DV:OFF_DZ + (h + 1) * DN_DV]
            o_n = o * row_rsqrt(o, 1.0 / DN_DV) * dn_norm_ref[...]
            mix_ref[:, h * DN_DV:(h + 1) * DN_DV] = (o_n * _silu(z)).astype(BF16)

    def gla():
        glow = small[:, SM_GLOW:SM_GLOW + GLA_RANK]
        zg = _dot(glow, gla_wup_ref[...]) + gla_bup_ref[...]
        gk2 = _softplus(-zg) * (-LOG2E / GLA_NORMALIZER)
        gcum = _dot_exact_lhs(tril, _split3(gk2))
        yield
        gq = proj_ref[:, OFF_GQ:OFF_GQ + GLA_QK] * (GLA_DK ** -0.5)
        gkk = proj_ref[:, OFF_GK:OFF_GK + GLA_QK]
        gq_b, gkk_b = gq.astype(BF16), gkk.astype(BF16)
        gcum_p = _split3(gcum)
        gmid_all = _dot_exact_lhs(sel_ref[...], gcum_p)
        yield
        level_ops = []
        for lev in range(nlev):
            e = jnp.exp2(-jnp.abs(gcum - gmid_all[lev * c:(lev + 1) * c, :]))
            e_b = e.astype(BF16)
            level_ops.append((gq_b * e_b, gkk_b * e_b, lv == lev))
            if lev % 2 == 1:
                yield
        glast = per_seq_last(gcum)
        q_inter = gq * jnp.exp2(gcum)
        k_dec = gkk * jnp.exp2(glast - gcum)
        eye_dk = (lax.broadcasted_iota(jnp.int32, (GLA_DK, GLA_DK), 0)
                  == lax.broadcasted_iota(jnp.int32, (GLA_DK, GLA_DK), 1))
        ones_dk = jnp.ones((GLA_DK, GLA_DV), BF16)
        yield
        for h in range(GLA_HEADS):
            hs = slice(h * GLA_DK, (h + 1) * GLA_DK)
            v = proj_ref[:, OFF_GV + h * GLA_DV:OFF_GV + (h + 1) * GLA_DV]
            scores = jnp.where(diag, lax.dot_general(gq_b[:, hs], gkk_b[:, hs], NT_DIMS,
                                                     preferred_element_type=F32), 0.0)
            for q_t, k_t, mask in level_ops:
                s_lev = lax.dot_general(q_t[:, hs], k_t[:, hs], NT_DIMS, preferred_element_type=F32)
                scores = jnp.where(mask, s_lev, scores)
            yield
            o_parts = []
            for b, r in enumerate(seq_rows):
                s = gla0_ref[b, h] if single_chunk else glan_ref[b, h]
                o_parts.append(_dot(q_inter[r, hs], s))
                lam = jnp.where(eye_dk, jnp.exp2(glast[r.start:r.start + 1, hs]), 0.0)
                keep = _dot_exact_rhs(_split3(lam), ones_dk)
                glan_ref[b, h] = keep * s + _dot_tn(k_dec[r, hs], v[r])
            o = cat_rows(o_parts) + _dot(scores, v)
            yield
            z = proj_ref[:, OFF_GG + h * GLA_DV:OFF_GG + (h + 1) * GLA_DV]
            o_n = o * row_rsqrt(o, 1.0 / GLA_DV) * gla_norm_ref[...]
            mix_ref[:, DN_V + h * GLA_DV:DN_V + (h + 1) * GLA_DV] = (o_n * _silu(z)).astype(BF16)

    def ssd():
        expand = expand_ref[...]
        sp_p = _split2(sp)
        dt_x = _dot_exact_rhs(sp_p, expand)
        ecum_x = _dot_exact_rhs(e_cum_p, expand)
        dtrem_x = _dot_exact_rhs(_split2(sp * e_rem), expand)
        yield
        cum_cols = _dot_exact_rhs(cum_p, bc_ssd_ref[...])
        lane_lo = lax.broadcasted_iota(jnp.int32, (c, LANES), 1) < M2_HEADDIM
        row_lo = lax.broadcasted_iota(jnp.int32, (LANES, LANES), 0) < M2_HEADDIM
        yield
        m2_off = DN_V + GLA_V
        total = jnp.zeros((c, LANES), F32)
        gated = []
        for g in range(M2_GROUPS):
            bm, cm = mb_all[g], mc_all[g]
            bm_b, cm_b = bm.astype(BF16), cm.astype(BF16)
            cb = lax.dot_general(cm_b, bm_b, NT_DIMS, preferred_element_type=F32)
            yield
            for pp in range(M2_PAIRS_PER_GROUP):
                p = g * M2_PAIRS_PER_GROUP + pp
                ps = slice(p * LANES, (p + 1) * LANES)
                ha, hb = SM_MDT + 2 * p, SM_MDT + 2 * p + 1
                xh = mx_all[p]
                x_dt = xh * dt_x[:, ps]
                col_a, col_b = 2 * p * LANES, (2 * p + 1) * LANES
                dec_a = jnp.exp2(cum_cols[:, col_a:col_a + c] - cum_t[ha:ha + 1, :] + negi)
                dec_b = jnp.exp2(cum_cols[:, col_b:col_b + c] - cum_t[hb:hb + 1, :] + negi)
                m_pair = jnp.concatenate([cb * dec_a, cb * dec_b], axis=1)
                x_bd = jnp.concatenate([jnp.where(lane_lo, x_dt, 0.0), jnp.where(lane_lo, 0.0, x_dt)], axis=0)
                x_rem = (xh * dtrem_x[:, ps]).astype(BF16)
                inter = []
                for b, r in enumerate(seq_rows):
                    s = ssm0_ref[b, p] if single_chunk else ssmn_ref[b, p]
                    inter.append(lax.dot_general(cm_b[r], s.astype(BF16), NT_DIMS, preferred_element_type=F32))
                    keep = jnp.where(row_lo, e_last[r.start:r.start + 1, ha:ha + 1],
                                     e_last[r.start:r.start + 1, hb:hb + 1])
                    ssmn_ref[b, p] = keep * s + lax.dot_general(x_rem[r], bm_b[r], TN_DIMS,
                                                                preferred_element_type=F32)
                y = _dot(m_pair, x_bd) + cat_rows(inter) * ecum_x[:, ps] + m2_d_ref[:, ps] * xh
                z = proj_ref[:, OFF_MZ + p * LANES:OFF_MZ + (p + 1) * LANES]
                yz = y * _silu(z)
                total = total + yz * yz
                gated.append(yz)
                yield
        scale = row_rsqrt_of_squares(total, 1.0 / M2_DIM)
        for p, yz in enumerate(gated):
            ps = slice(p * LANES, (p + 1) * LANES)
            mix_ref[:, m2_off + p * LANES:m2_off + (p + 1) * LANES] = (yz * scale * m2_norm_ref[:, ps]).astype(BF16)

    _round_robin((ssd(), 1), (gla(), 1), (deltanet(), 1))

    @at_last_chunk
    def _():
        convn_ref[...] = cbuf[:, pl.ds(CONV_PAD + seq_len - CONV_TAIL, CONV_TAIL), :]

    if not single_chunk:
        cbuf[:, pl.ds(0, CONV_PAD), :] = cbuf[:, pl.ds(seq_len, CONV_PAD), :]


def _mixer(proj, row_start, nb, seq, nseq, seq_len, layer, conv0, dn0, gla0, ssm0, params,
           total_rows, prev_mix=None, prev_states=None):
    c = nseq * seq_len
    nc = seq // seq_len
    blk0 = row_start // c
    lsel = layer if conv0.shape[0] > 1 else 0
    consts, nlev, nmerge, nsquare = _mixer_constants(nseq, seq_len)
    const_args = [consts[k] for k in CONST_NAMES]

    def state_spec(shape):
        zeros = (0,) * len(shape)
        return pl.BlockSpec((None, nseq) + shape, lambda b, k: (lsel, b) + zeros)

    def out_state_spec(shape):
        zeros = (0,) * len(shape)
        return pl.BlockSpec((None, nseq) + shape, lambda b, k: (layer, b) + zeros)

    conv_shape = (CONV_TAIL, CONV_CH)
    dn_shape = (DN_HEADS, DN_DK, DN_DV)
    gla_shape = (GLA_HEADS, GLA_DK, GLA_DV)
    ssm_shape = (M2_PAIRS, 2 * M2_HEADDIM, M2_STATE)
    state_shapes = (conv_shape, dn_shape, gla_shape, ssm_shape)
    in_specs = [pl.BlockSpec((c, PROJ_COLS), lambda b, k: (blk0 + b * nc + k, 0))]
    in_specs += [state_spec(s) for s in state_shapes]
    in_specs += [_resident(p.shape) for p in params]
    in_specs += [_resident(a.shape) for a in const_args]
    args = [proj, conv0, dn0, gla0, ssm0, *params, *const_args]
    aliases = {}
    if prev_mix is not None:
        aliases[len(args)] = 0
        args.append(prev_mix)
    if prev_states is not None:
        for i, st in enumerate(prev_states):
            aliases[len(args)] = 1 + i
            args.append(st)
    n_alias = len(aliases)
    in_specs += [pl.BlockSpec(memory_space=pl.ANY)] * n_alias
    out_specs = [pl.BlockSpec((c, D_MIX), lambda b, k: (blk0 + b * nc + k, 0))]
    out_specs += [out_state_spec(s) for s in state_shapes]
    out_shape = [jax.ShapeDtypeStruct((total_rows, D_MIX), BF16)]
    out_shape += [jax.ShapeDtypeStruct((DEPTH, nb) + s, F32) for s in state_shapes]
    body = functools.partial(_mixer_kernel, nseq, seq_len, nlev, nmerge, nsquare, nc == 1, n_alias)
    return pl.pallas_call(
        body, grid=(nb // nseq, nc), in_specs=in_specs, out_specs=out_specs, out_shape=out_shape,
        input_output_aliases=aliases,
        scratch_shapes=[pltpu.VMEM((nseq, CONV_PAD + seq_len, CONV_CH), F32), pltpu.VMEM((c, CONV_CH), F32)],
        compiler_params=pltpu.CompilerParams(dimension_semantics=("parallel", "arbitrary"),
                                             vmem_limit_bytes=VMEM_LIMIT_BYTES),
        name=f"mixer_{nseq}x{seq_len}")(*args)


def _permute_w_in(wt):
    o = CONV_CH
    dz = wt[:, o:o + DN_V]; o += DN_V
    da = wt[:, o:o + DN_HEADS]; o += DN_HEADS
    db = wt[:, o:o + DN_HEADS]; o += DN_HEADS
    gq = wt[:, o:o + GLA_QK]; o += GLA_QK
    gk = wt[:, o:o + GLA_QK]; o += GLA_QK
    gv = wt[:, o:o + GLA_V]; o += GLA_V
    glow = wt[:, o:o + GLA_RANK]; o += GLA_RANK
    gg = wt[:, o:o + GLA_V]; o += GLA_V
    mz = wt[:, o:o + M2_DIM]; o += M2_DIM
    mdt = wt[:, o:o + M2_HEADS]; o += M2_HEADS
    assert o == wt.shape[1]
    pad = jnp.zeros((wt.shape[0], LANES - SM_USED, wt.shape[2]), wt.dtype)
    return jnp.concatenate([wt[:, :CONV_CH], dz, gq, gk, gv, gg, mz, da, mdt, db, glow, pad], axis=1)


def _small_lane_row(dn_vals, m2_vals):
    pad = jnp.zeros((LANES - SM_DB,), F32)
    return jnp.concatenate([dn_vals.astype(F32), m2_vals.astype(F32), pad]).reshape(1, LANES)


def _ssm_to_pairs(s):
    t = jnp.swapaxes(s, -1, -2)
    return t.reshape(t.shape[:-3] + (M2_PAIRS, 2 * M2_HEADDIM, M2_STATE))


def _ssm_from_pairs(s):
    t = s.reshape(s.shape[:-3] + (M2_HEADS, M2_HEADDIM, M2_STATE))
    return jnp.swapaxes(t, -1, -2)


def kernel(x_prompt, x_sample, state_conv, state_delta, state_gla, state_ssm, ffn1_norm, ffn1_w_gu, ffn1_w_down, mix_norm, w_in, conv_w, conv_b, dn_a_log, dn_dt_bias, dn_norm, gla_w_up, gla_b_up, gla_norm, m2_a_log, m2_dt_bias, m2_d, m2_norm, w_out, ffn2_norm, ffn2_w_gu, ffn2_w_down, final_norm):
    bp, sp, _ = x_prompt.shape
    bs, ss, _ = x_sample.shape
    tp, ts = bp * sp, bs * ss
    assert tp % DENSE_TILE == 0 and ts % DENSE_TILE == 0 and sp % PROMPT_CHUNK == 0
    assert bs % SAMPLE_SEQS_PER_STEP == 0 and tp % (SAMPLE_SEQS_PER_STEP * ss) == 0 and ss % SUBLANES == 0

    zero_conv = jnp.zeros((1, bp, CONV_TAIL, CONV_CH), F32)
    zero_dn = jnp.zeros((1, bp, DN_HEADS, DN_DK, DN_DV), F32)
    zero_gla = jnp.zeros((1, bp, GLA_HEADS, GLA_DK, GLA_DV), F32)
    zero_ssm = jnp.zeros((1, bp, M2_PAIRS, 2 * M2_HEADDIM, M2_STATE), F32)
    ssm_pairs = _ssm_to_pairs(state_ssm)

    wgu1, wd1, wgu2, wd2, wout = ffn1_w_gu, ffn1_w_down, ffn2_w_gu, ffn2_w_down, w_out
    win = _permute_w_in(jnp.swapaxes(w_in, 1, 2).astype(BF16))
    g_ffn1, g_mix, g_ffn2 = (g.reshape(DEPTH, 1, D_MODEL) for g in (ffn1_norm, mix_norm, ffn2_norm))

    xs = [x_prompt.reshape(tp, D_MODEL), x_sample.reshape(ts, D_MODEL)]
    st_p = st_s = None
    for l in range(DEPTH):
        params = (conv_w[l], conv_b[l].reshape(1, CONV_CH),
                  _small_lane_row(dn_a_log[l], m2_a_log[l]), _small_lane_row(dn_dt_bias[l], m2_dt_bias[l]),
                  dn_norm[l].reshape(1, DN_DV),
                  gla_w_up[l].astype(BF16), gla_b_up[l].reshape(1, GLA_QK), gla_norm[l].reshape(1, GLA_DV),
                  jnp.repeat(m2_d[l], M2_HEADDIM).reshape(1, M2_DIM), m2_norm[l].reshape(1, M2_DIM))

        x = _ffn(xs, l, g_ffn1, wgu1, wd1)
        proj = _inproj(x, l, g_mix, win)
        mix, *st_p = _mixer(proj, 0, bp, sp, 1, PROMPT_CHUNK, l, zero_conv, zero_dn, zero_gla, zero_ssm, params,
                            tp + ts, prev_states=st_p)
        mix, *st_s = _mixer(proj, tp, bs, ss, SAMPLE_SEQS_PER_STEP, ss, l,
                            state_conv, state_delta, state_gla, ssm_pairs, params,
                            tp + ts, prev_mix=mix, prev_states=st_s)
        if l < DEPTH - 1:
            xs = [_ffn([x], l, g_ffn2, wgu2, wd2, mix=mix, wout=wout)]
        else:
            y_prompt, y_sample = _ffn([x], l, g_ffn2, wgu2, wd2, final_gain=final_norm, split_rows=tp,
                                      mix=mix, wout=wout)

    return (y_prompt.reshape(bp, sp, D_MODEL), y_sample.reshape(bs, ss, D_MODEL),
            st_p[0], st_p[1], st_p[2], _ssm_from_pairs(st_p[3]),
            st_s[0], st_s[1], st_s[2], _ssm_from_pairs(st_s[3]))
```

```python
import functools
import math

import jax
import jax.numpy as jnp
import numpy as np
from jax import lax
from jax.experimental import pallas as pl
from jax.experimental.pallas import tpu as pltpu

F32 = jnp.float32
BF16 = jnp.bfloat16
LOG2E = math.log2(math.e)

D_MODEL = 1024
DEPTH = 2
D_MIX = 2 * D_MODEL
D_FF = 2816
CONV_W = 4
DN_HEADS = 4
DN_DK = 128
DN_DV = 128
GLA_HEADS = 4
GLA_DK = 64
GLA_DV = 128
GLA_RANK = 16
GLA_NORMALIZER = 16.0
M2_DIM = 1024
M2_HEADDIM = 64
M2_HEADS = 16
M2_STATE = 128
M2_GROUPS = 2
M2_PAIRS = M2_HEADS // 2
M2_PAIRS_PER_GROUP = M2_PAIRS // M2_GROUPS
EPS = 1e-6
LANES = 128
SUBLANES = 8

DN_QK = DN_HEADS * DN_DK
DN_V = DN_HEADS * DN_DV
GLA_QK = GLA_HEADS * GLA_DK
GLA_V = GLA_HEADS * GLA_DV
M2_BC = M2_GROUPS * M2_STATE
CONV_CH = 3 * 512 + M2_DIM + 2 * M2_BC

OFF_DQ = 0
OFF_DK = OFF_DQ + DN_QK
OFF_DV = OFF_DK + DN_QK
OFF_MX = OFF_DV + DN_V
OFF_MB = OFF_MX + M2_DIM
OFF_MC = OFF_MB + M2_BC
OFF_DZ = CONV_CH
OFF_GQ = OFF_DZ + DN_V
OFF_GK = OFF_GQ + GLA_QK
OFF_GV = OFF_GK + GLA_QK
OFF_GG = OFF_GV + GLA_V
OFF_MZ = OFF_GG + GLA_V
OFF_SMALL = OFF_MZ + M2_DIM
SM_DA = 0
SM_MDT = SM_DA + DN_HEADS
SM_DB = SM_MDT + M2_HEADS
SM_GLOW = SM_DB + DN_HEADS
SM_USED = SM_GLOW + GLA_RANK
PROJ_COLS = OFF_SMALL + LANES

CONV_PAD = SUBLANES
CONV_TAIL = CONV_W - 1
NEUMANN_BLOCK = 16
VMEM_LIMIT_BYTES = 56 * 1024 * 1024
PROMPT_CHUNK = 128
SAMPLE_SEQS_PER_STEP = 8
DENSE_TILE = 256
WEIGHT_CHUNK = 512
WEIGHT_ROWS = 256
TRANSPOSE_ROWS = 768
MASKED = -1e30
NT_DIMS = (((1,), (1,)), ((), ()))
TN_DIMS = (((0,), (0,)), ((), ()))


def _dot(a, b):
    return jnp.dot(a.astype(BF16), b.astype(BF16), preferred_element_type=F32)


def _dot_nt(a, b):
    return lax.dot_general(a.astype(BF16), b.astype(BF16), NT_DIMS, preferred_element_type=F32)


def _dot_tn(a, b):
    return lax.dot_general(a.astype(BF16), b.astype(BF16), TN_DIMS, preferred_element_type=F32)


def _split2(x):
    hi = x.astype(BF16)
    return hi, (x - hi.astype(F32)).astype(BF16)


def _split3(x):
    hi = x.astype(BF16)
    r1 = x - hi.astype(F32)
    mid = r1.astype(BF16)
    lo = (r1 - mid.astype(F32)).astype(BF16)
    return hi, mid, lo


def _sum_dots(pieces, fn):
    acc = None
    for piece in pieces:
        part = fn(piece)
        acc = part if acc is None else acc + part
    return acc


def _dot_exact_lhs(m, pieces):
    return _sum_dots(pieces, lambda p: jnp.dot(m, p, preferred_element_type=F32))


def _dot_exact_rhs(pieces, m):
    return _sum_dots(pieces, lambda p: jnp.dot(p, m, preferred_element_type=F32))


def _dot_exact_tn(pieces, m):
    return _sum_dots(pieces, lambda p: lax.dot_general(p, m, TN_DIMS, preferred_element_type=F32))


def _silu(x):
    return x * jax.nn.sigmoid(x)


def _softplus(x):
    return jnp.maximum(x, 0.0) + jnp.log(1.0 + jnp.exp(-jnp.abs(x)))


def _rms(x, gain):
    return x * lax.rsqrt(jnp.mean(x * x, axis=-1, keepdims=True) + EPS) * gain


def _stream_cast(chunks, stage, sem):
    def copy(j):
        return pltpu.make_async_copy(chunks[j][0], stage.at[j % 2], sem.at[j % 2])

    copy(0).start()
    for j, (_, dst) in enumerate(chunks):
        if j + 1 < len(chunks):
            copy(j + 1).start()
        copy(j).wait()
        dst[...] = stage[j % 2].astype(BF16)


def _ffn_kernel(layer, split_tile, final_norm, with_outproj, *refs):
    refs = list(refs)
    two_in = split_tile is not None and not final_norm
    two_out = split_tile is not None and final_norm
    n_out = 2 if two_out else 1
    x_refs = [refs.pop(0) for _ in range(2 if two_in else 1)]
    mix_ref, wout_hbm = (refs.pop(0), refs.pop(0)) if with_outproj else (None, None)
    g_ref, wgu_hbm, wd_hbm = (refs.pop(0) for _ in range(3))
    fg_ref = refs.pop(0) if final_norm else None
    o_refs = [refs.pop(0) for _ in range(n_out)]
    wgu_ref, wd_ref = refs.pop(0), refs.pop(0)
    wout_ref = refs.pop(0) if with_outproj else None
    stage_cols, stage_rows, sem = refs
    i = pl.program_id(0)

    @pl.when(i == 0)
    def _():
        cols = [(wgu_hbm.at[layer, :, pl.ds(j * WEIGHT_CHUNK, WEIGHT_CHUNK)],
                 wgu_ref.at[:, pl.ds(j * WEIGHT_CHUNK, WEIGHT_CHUNK)]) for j in range(2 * D_FF // WEIGHT_CHUNK)]
        _stream_cast(cols, stage_cols, sem)
        rows = [(wd_hbm.at[layer, pl.ds(j * WEIGHT_ROWS, WEIGHT_ROWS), :],
                 wd_ref.at[pl.ds(j * WEIGHT_ROWS, WEIGHT_ROWS), :]) for j in range(D_FF // WEIGHT_ROWS)]
        if with_outproj:
            rows += [(wout_hbm.at[layer, pl.ds(j * WEIGHT_ROWS, WEIGHT_ROWS), :],
                      wout_ref.at[pl.ds(j * WEIGHT_ROWS, WEIGHT_ROWS), :]) for j in range(D_MIX // WEIGHT_ROWS)]
        _stream_cast(rows, stage_rows, sem)

    x = jnp.where(i < split_tile, x_refs[0][...], x_refs[1][...]) if two_in else x_refs[0][...]
    if with_outproj:
        x = x + jnp.dot(mix_ref[...], wout_ref[...], preferred_element_type=F32)
    xn = _rms(x, g_ref[...]).astype(BF16)
    gate = jnp.dot(xn, wgu_ref[:, :D_FF], preferred_element_type=F32)
    up = jnp.dot(xn, wgu_ref[:, D_FF:], preferred_element_type=F32)
    h = (_silu(gate) * up).astype(BF16)
    y = x + 0.5 * jnp.dot(h, wd_ref[...], preferred_element_type=F32)
    if final_norm:
        y = _rms(y, fg_ref[...])
    if two_out:
        @pl.when(i < split_tile)
        def _():
            o_refs[0][...] = y

        @pl.when(i >= split_tile)
        def _():
            o_refs[1][...] = y
    else:
        o_refs[0][...] = y


def _inproj_kernel(x_ref, g_ref, wt_ref, wsmall_ref, o_ref, w_ref):
    @pl.when(pl.program_id(0) == 0)
    def _():
        for r in range(0, OFF_SMALL, TRANSPOSE_ROWS):
            w_ref[:, r:r + TRANSPOSE_ROWS] = wt_ref[r:r + TRANSPOSE_ROWS, :].T
        w_ref[:, OFF_SMALL:PROJ_COLS] = wsmall_ref[...].T

    xn = _rms(x_ref[...], g_ref[...]).astype(BF16)
    o_ref[...] = jnp.dot(xn, w_ref[...], preferred_element_type=F32)


def _resident(shape):
    nd = len(shape)
    return pl.BlockSpec(shape, lambda *_: (0,) * nd, pipeline_mode=pl.Buffered(1))


def _layer_resident(arr, layer):
    zeros = (0,) * (arr.ndim - 1)
    return pl.BlockSpec((None,) + arr.shape[1:], lambda *_: (layer,) + zeros, pipeline_mode=pl.Buffered(1))


def _ffn(xs, layer, gains, wgu, wd, final_gain=None, split_rows=None, mix=None, wout=None):
    assert (2 * D_FF) % WEIGHT_CHUNK == 0 and D_FF % WEIGHT_ROWS == 0 and D_MIX % WEIGHT_ROWS == 0
    t = sum(x.shape[0] for x in xs)
    two_in = len(xs) == 2
    two_out = final_gain is not None and split_rows is not None
    split_tile = None
    if two_in or two_out:
        split_tile = (xs[0].shape[0] if two_in else split_rows) // DENSE_TILE
    tile = pl.BlockSpec((DENSE_TILE, D_MODEL), lambda i: (i, 0))
    first = pl.BlockSpec((DENSE_TILE, D_MODEL), lambda i: (jnp.minimum(i, split_tile - 1), 0))
    second = pl.BlockSpec((DENSE_TILE, D_MODEL), lambda i: (jnp.maximum(i - split_tile, 0), 0))
    in_specs = [first, second] if two_in else [tile]
    args = list(xs)
    hbm = pl.BlockSpec(memory_space=pl.ANY)
    scratch = [pltpu.VMEM((D_MODEL, 2 * D_FF), BF16), pltpu.VMEM((D_FF, D_MODEL), BF16)]
    if mix is not None:
        in_specs += [pl.BlockSpec((DENSE_TILE, D_MIX), lambda i: (i, 0)), hbm]
        args += [mix, wout]
        scratch.append(pltpu.VMEM((D_MIX, D_MODEL), BF16))
    scratch += [pltpu.VMEM((2, D_MODEL, WEIGHT_CHUNK), F32), pltpu.VMEM((2, WEIGHT_ROWS, D_MODEL), F32),
                pltpu.SemaphoreType.DMA((2,))]
    in_specs += [_layer_resident(gains, layer), hbm, hbm]
    args += [gains, wgu, wd]
    if final_gain is not None:
        in_specs.append(_resident((1, D_MODEL)))
        args.append(final_gain.reshape(1, D_MODEL))
    if two_out:
        out_specs = [first, second]
        out_shape = [jax.ShapeDtypeStruct((split_rows, D_MODEL), F32),
                     jax.ShapeDtypeStruct((t - split_rows, D_MODEL), F32)]
    else:
        out_specs, out_shape = tile, jax.ShapeDtypeStruct((t, D_MODEL), F32)
    return pl.pallas_call(
        functools.partial(_ffn_kernel, layer, split_tile, final_gain is not None, mix is not None),
        grid=(t // DENSE_TILE,), in_specs=in_specs, out_specs=out_specs, out_shape=out_shape,
        scratch_shapes=scratch,
        compiler_params=pltpu.CompilerParams(dimension_semantics=("arbitrary",),
                                             vmem_limit_bytes=VMEM_LIMIT_BYTES),
        name="ffn" if final_gain is None else "ffn_final_norm")(*args)


def _inproj(x, layer, gains, wt, wsmall):
    t = x.shape[0]
    assert OFF_SMALL % TRANSPOSE_ROWS == 0
    return pl.pallas_call(
        _inproj_kernel, grid=(t // DENSE_TILE,),
        in_specs=[pl.BlockSpec((DENSE_TILE, D_MODEL), lambda i: (i, 0)), _layer_resident(gains, layer),
                  _layer_resident(wt, layer), _layer_resident(wsmall, layer)],
        out_specs=pl.BlockSpec((DENSE_TILE, PROJ_COLS), lambda i: (i, 0)),
        out_shape=jax.ShapeDtypeStruct((t, PROJ_COLS), F32),
        scratch_shapes=[pltpu.VMEM((D_MODEL, PROJ_COLS), BF16)],
        compiler_params=pltpu.CompilerParams(dimension_semantics=("arbitrary",), vmem_limit_bytes=VMEM_LIMIT_BYTES),
        name="inproj")(x, gains, wt, wsmall)


CONST_NAMES = ("tril", "negi", "blk", "merge", "sel", "lv", "expand", "bc_dn", "bc_beta", "bc_ssd")


def _mixer_constants(nseq, seq_len):
    c = nseq * seq_len
    idx = np.arange(c)
    seq, pos = idx // seq_len, idx % seq_len
    same = seq[:, None] == seq[None, :]
    pi, pj = pos[:, None], pos[None, :]
    causal = same & (pi >= pj)
    base = min(seq_len, NEUMANN_BLOCK)
    merges = []
    s = base
    while s < seq_len:
        merges.append(same & (pi // (2 * s) == pj // (2 * s)) & (pi // s != pj // s))
        s *= 2
    levels = []
    m = seq_len // 2
    while m >= 1:
        levels.append(m)
        m //= 2
    sel = np.zeros((len(levels), c, c), np.float32)
    lv = np.full((c, c), -1, np.int32)
    xor = pi ^ pj
    for k, m in enumerate(levels):
        mid = seq * seq_len + (pos // (2 * m)) * (2 * m) + (m - 1)
        sel[k, idx, mid] = 1.0
        lv[same & (pi > pj) & (xor >= m) & (xor < 2 * m)] = k
    lv[idx, idx] = len(levels)
    expand = np.zeros((LANES, M2_DIM), np.float32)
    bc_ssd = np.zeros((LANES, M2_HEADS * LANES), np.float32)
    for h in range(M2_HEADS):
        expand[SM_MDT + h, h * M2_HEADDIM:(h + 1) * M2_HEADDIM] = 1.0
        bc_ssd[SM_MDT + h, h * LANES:(h + 1) * LANES] = 1.0
    bc_dn = np.zeros((LANES, DN_HEADS * LANES), np.float32)
    bc_beta = np.zeros((LANES, DN_HEADS * LANES), np.float32)
    for h in range(DN_HEADS):
        bc_dn[SM_DA + h, h * LANES:(h + 1) * LANES] = 1.0
        bc_beta[SM_DB + h, h * LANES:(h + 1) * LANES] = 1.0
    consts = dict(
        tril=jnp.asarray(causal, BF16),
        negi=jnp.asarray(np.where(causal, 0.0, MASKED), F32),
        blk=jnp.asarray(same & (pi // base == pj // base), F32),
        merge=jnp.asarray(np.stack(merges) if merges else np.zeros((1, SUBLANES, LANES)), F32),
        sel=jnp.asarray(sel.reshape(len(levels) * c, c), BF16),
        lv=jnp.asarray(lv),
        expand=jnp.asarray(expand, BF16),
        bc_dn=jnp.asarray(bc_dn, BF16),
        bc_beta=jnp.asarray(bc_beta, BF16),
        bc_ssd=jnp.asarray(bc_ssd, BF16),
    )
    nsquare = int(math.log2(base)) - 1
    return consts, len(levels), len(merges), nsquare


def _round_robin(*staged):
    pending = list(staged)
    while pending:
        for item in list(pending):
            g, per_turn = item
            try:
                for _ in range(per_turn):
                    next(g)
            except StopIteration:
                pending.remove(item)


def _mixer_kernel(nseq, seq_len, nlev, nmerge, nsquare, single_chunk, n_alias,
                  proj_ref, conv0_ref, dn0_ref, gla0_ref, ssm0_ref,
                  convw_ref, convb_ref, alog_ref, dtb_ref, dn_norm_ref, gla_wup_ref, gla_bup_ref, gla_norm_ref,
                  m2_d_ref, m2_norm_ref,
                  tril_ref, negi_ref, blk_ref, merge_ref, sel_ref, lv_ref, expand_ref, bc_dn_ref, bc_beta_ref,
                  bc_ssd_ref, *rest):
    mix_ref, convn_ref, dnn_ref, glan_ref, ssmn_ref, cbuf, cvout = rest[n_alias:]
    c = nseq * seq_len
    chunk = pl.program_id(1)
    last_chunk = pl.num_programs(1) - 1
    seq_rows = [slice(b * seq_len, (b + 1) * seq_len) for b in range(nseq)]

    def at_first_chunk(fn):
        if single_chunk:
            fn()
        else:
            pl.when(chunk == 0)(fn)

    def at_last_chunk(fn):
        if single_chunk:
            fn()
        else:
            pl.when(chunk == last_chunk)(fn)

    @at_first_chunk
    def _():
        cbuf[:, pl.ds(CONV_PAD - CONV_TAIL, CONV_TAIL), :] = conv0_ref[...]
        if not single_chunk:
            dnn_ref[...] = dn0_ref[...]
            glan_ref[...] = gla0_ref[...]
            ssmn_ref[...] = ssm0_ref[...]

    cbuf[:, pl.ds(CONV_PAD, seq_len), :] = proj_ref[:, 0:CONV_CH].reshape(nseq, seq_len, CONV_CH)

    def conv(lo, width):
        acc = convb_ref[:, lo:lo + width].reshape(1, 1, width)
        for w in range(CONV_W):
            tap = convw_ref[w:w + 1, lo:lo + width].reshape(1, 1, width)
            acc = acc + tap * cbuf[:, pl.ds(CONV_PAD - CONV_TAIL + w, seq_len), lo:lo + width]
        return _silu(acc).reshape(c, width)

    for lo in range(0, CONV_CH, 512):
        cvout[:, lo:lo + 512] = conv(lo, 512)
    dq_all = cvout[:, OFF_DQ:OFF_DQ + DN_QK]
    dk_all = cvout[:, OFF_DK:OFF_DK + DN_QK]
    dv_all = cvout[:, OFF_DV:OFF_DV + DN_V]
    mx_all = [cvout[:, OFF_MX + p * LANES:OFF_MX + (p + 1) * LANES] for p in range(M2_PAIRS)]
    mb_all = [cvout[:, OFF_MB + g * M2_STATE:OFF_MB + (g + 1) * M2_STATE] for g in range(M2_GROUPS)]
    mc_all = [cvout[:, OFF_MC + g * M2_STATE:OFF_MC + (g + 1) * M2_STATE] for g in range(M2_GROUPS)]

    ones_sq = jnp.ones((LANES, LANES), BF16)

    def row_rsqrt(x, mult):
        return row_rsqrt_of_squares(x * x, mult)

    def row_rsqrt_of_squares(sq, mult):
        sums = jnp.dot(sq.astype(BF16), ones_sq, preferred_element_type=F32)
        return lax.rsqrt(sums * mult + EPS)

    def per_seq_last(x):
        parts = [jnp.broadcast_to(x[r.stop - 1:r.stop, :], (seq_len, x.shape[1])) for r in seq_rows]
        return parts[0] if nseq == 1 else jnp.concatenate(parts, axis=0)

    def cat_rows(parts):
        return parts[0] if nseq == 1 else jnp.concatenate(parts, axis=0)

    tril = tril_ref[...]
    negi = negi_ref[...]
    lv = lv_ref[...]
    diag = lv == nlev

    small = proj_ref[:, OFF_SMALL:OFF_SMALL + LANES]
    sp = _softplus(small + dtb_ref[...])
    g2 = (-LOG2E * jnp.exp(alog_ref[...])) * sp
    cum = _dot_exact_lhs(tril, _split3(g2))
    cum_t = cum.T
    last = per_seq_last(cum)
    e_cum = jnp.exp2(cum)
    e_rem = jnp.exp2(last - cum)
    e_last = jnp.exp2(last)
    cum_p, e_cum_p, e_rem_p = _split3(cum), _split2(e_cum), _split2(e_rem)

    def deltanet():
        bc_dn = bc_dn_ref[...]
        cum_col = _dot_exact_rhs(cum_p, bc_dn)
        egc_col = _dot_exact_rhs(e_cum_p, bc_dn)
        erem_col = _dot_exact_rhs(e_rem_p, bc_dn)
        beta_col = _dot_exact_rhs(_split2(jax.nn.sigmoid(small)), bc_beta_ref[...])
        yield
        dn = []
        for h in range(DN_HEADS):
            hs = slice(h * DN_DK, (h + 1) * DN_DK)
            dq, dk = dq_all[:, hs], dk_all[:, hs]
            st = dict(q=dq * (row_rsqrt(dq, 1.0) * (DN_DK ** -0.5)), k=dk * row_rsqrt(dk, 1.0),
                      v=dv_all[:, hs], beta=beta_col[:, hs], egc=egc_col[:, hs], erem=erem_col[:, hs])
            dn.append(st)
        yield
        for h, st in enumerate(dn):
            kb, qb = st["k"].astype(BF16), st["q"].astype(BF16)
            decay = jnp.exp2(cum_col[:, h * LANES:h * LANES + c] - cum_t[SM_DA + h:SM_DA + h + 1, :] + negi)
            kk = lax.dot_general(kb, kb, NT_DIMS, preferred_element_type=F32)
            qk = lax.dot_general(qb, kb, NT_DIMS, preferred_element_type=F32)
            st["a"] = jnp.where(diag, 0.0, kk * decay * st["beta"][:, :c])
            st["qkd"] = qk * decay
        yield
        blk = blk_ref[...]
        for st in dn:
            st["p"] = st["a"] * blk
            st["n"] = -st["p"]
        for _ in range(nsquare):
            for st in dn:
                st["p"] = _dot(st["p"], st["p"])
            yield
            for st in dn:
                st["n"] = st["n"] + st["p"] + _dot(st["n"], st["p"])
            yield
        for lvl in range(nmerge):
            emask = merge_ref[lvl]
            for st in dn:
                e = st["a"] * emask
                st["y"] = e + _dot(e, st["n"])
            yield
            for st in dn:
                st["n"] = st["n"] - st["y"] - _dot(st["n"], st["y"])
            yield
        for st in dn:
            rhs = jnp.concatenate([st["k"] * (st["beta"] * st["egc"]), st["v"] * st["beta"]], axis=1)
            st["sol"] = rhs + _dot(st["n"], rhs)
        yield
        for h, st in enumerate(dn):
            sol = st["sol"]
            w_mat, u_mat = sol[:, :DN_DK], sol[:, DN_DK:]
            qe = st["q"] * st["egc"]
            v_parts, o_parts, states = [], [], []
            for b, r in enumerate(seq_rows):
                s = dn0_ref[b, h] if single_chunk else dnn_ref[b, h]
                both = _dot(jnp.concatenate([w_mat[r], qe[r]], axis=0), s)
                v_parts.append(u_mat[r] - both[:seq_len])
                o_parts.append(both[seq_len:])
                states.append(s)
            st["v_new"], st["o"], st["s"] = cat_rows(v_parts), cat_rows(o_parts), states
        yield
        for h, st in enumerate(dn):
            v_new = st["v_new"]
            o = st["o"] + _dot(st["qkd"], v_new)
            kdec = st["k"] * st["erem"]
            lane = SM_DA + h
            for b, r in enumerate(seq_rows):
                dnn_ref[b, h] = (e_last[r.start:r.start + 1, lane:lane + 1] * st["s"][b]
                                 + _dot_tn(kdec[r], v_new[r]))
            st["o"] = o
        yield
        for h, st in enumerate(dn):
            o = st["o"]
            z = proj_ref[:, OFF_DZ + h * DN_DV:OFF_DZ + (h + 1) * DN_DV]
            o_n = o * row_rsqrt(o, 1.0 / DN_DV) * dn_norm_ref[...]
            mix_ref[:, h * DN_DV:(h + 1) * DN_DV] = (o_n * _silu(z)).astype(BF16)

    def gla():
        glow = small[:, SM_GLOW:SM_GLOW + GLA_RANK]
        zg = _dot(glow, gla_wup_ref[...]) + gla_bup_ref[...]
        gk2 = _softplus(-zg) * (-LOG2E / GLA_NORMALIZER)
        gcum = _dot_exact_lhs(tril, _split3(gk2))
        yield
        gq = proj_ref[:, OFF_GQ:OFF_GQ + GLA_QK] * (GLA_DK ** -0.5)
        gkk = proj_ref[:, OFF_GK:OFF_GK + GLA_QK]
        gq_b, gkk_b = gq.astype(BF16), gkk.astype(BF16)
        gcum_p = _split3(gcum)
        gmid_all = _dot_exact_lhs(sel_ref[...], gcum_p)
        yield
        level_ops = []
        for lev in range(nlev):
            e = jnp.exp2(-jnp.abs(gcum - gmid_all[lev * c:(lev + 1) * c, :]))
            e_b = e.astype(BF16)
            level_ops.append((gq_b * e_b, gkk_b * e_b, lv == lev))
            if lev % 2 == 1:
                yield
        glast = per_seq_last(gcum)
        q_inter = gq * jnp.exp2(gcum)
        k_dec = gkk * jnp.exp2(glast - gcum)
        eye_dk = (lax.broadcasted_iota(jnp.int32, (GLA_DK, GLA_DK), 0)
                  == lax.broadcasted_iota(jnp.int32, (GLA_DK, GLA_DK), 1))
        ones_dk = jnp.ones((GLA_DK, GLA_DV), BF16)
        yield
        for h in range(GLA_HEADS):
            hs = slice(h * GLA_DK, (h + 1) * GLA_DK)
            v = proj_ref[:, OFF_GV + h * GLA_DV:OFF_GV + (h + 1) * GLA_DV]
            scores = jnp.where(diag, lax.dot_general(gq_b[:, hs], gkk_b[:, hs], NT_DIMS,
                                                     preferred_element_type=F32), 0.0)
            for q_t, k_t, mask in level_ops:
                s_lev = lax.dot_general(q_t[:, hs], k_t[:, hs], NT_DIMS, preferred_element_type=F32)
                scores = jnp.where(mask, s_lev, scores)
            yield
            o_parts = []
            for b, r in enumerate(seq_rows):
                s = gla0_ref[b, h] if single_chunk else glan_ref[b, h]
                o_parts.append(_dot(q_inter[r, hs], s))
                lam = jnp.where(eye_dk, jnp.exp2(glast[r.start:r.start + 1, hs]), 0.0)
                keep = _dot_exact_rhs(_split3(lam), ones_dk)
                glan_ref[b, h] = keep * s + _dot_tn(k_dec[r, hs], v[r])
            o = cat_rows(o_parts) + _dot(scores, v)
            yield
            z = proj_ref[:, OFF_GG + h * GLA_DV:OFF_GG + (h + 1) * GLA_DV]
            o_n = o * row_rsqrt(o, 1.0 / GLA_DV) * gla_norm_ref[...]
            mix_ref[:, DN_V + h * GLA_DV:DN_V + (h + 1) * GLA_DV] = (o_n * _silu(z)).astype(BF16)

    def ssd():
        expand = expand_ref[...]
        sp_p = _split2(sp)
        dt_x = _dot_exact_rhs(sp_p, expand)
        ecum_x = _dot_exact_rhs(e_cum_p, expand)
        dtrem_x = _dot_exact_rhs(_split2(sp * e_rem), expand)
        yield
        cum_cols = _dot_exact_rhs(cum_p, bc_ssd_ref[...])
        lane_lo = lax.broadcasted_iota(jnp.int32, (c, LANES), 1) < M2_HEADDIM
        row_lo = lax.broadcasted_iota(jnp.int32, (LANES, LANES), 0) < M2_HEADDIM
        yield
        m2_off = DN_V + GLA_V
        total = jnp.zeros((c, LANES), F32)
        gated = []
        for g in range(M2_GROUPS):
            bm, cm = mb_all[g], mc_all[g]
            bm_b, cm_b = bm.astype(BF16), cm.astype(BF16)
            cb = lax.dot_general(cm_b, bm_b, NT_DIMS, preferred_element_type=F32)
            yield
            for pp in range(M2_PAIRS_PER_GROUP):
                p = g * M2_PAIRS_PER_GROUP + pp
                ps = slice(p * LANES, (p + 1) * LANES)
                ha, hb = SM_MDT + 2 * p, SM_MDT + 2 * p + 1
                xh = mx_all[p]
                x_dt = xh * dt_x[:, ps]
                col_a, col_b = 2 * p * LANES, (2 * p + 1) * LANES
                dec_a = jnp.exp2(cum_cols[:, col_a:col_a + c] - cum_t[ha:ha + 1, :] + negi)
                dec_b = jnp.exp2(cum_cols[:, col_b:col_b + c] - cum_t[hb:hb + 1, :] + negi)
                m_pair = jnp.concatenate([cb * dec_a, cb * dec_b], axis=1)
                x_bd = jnp.concatenate([jnp.where(lane_lo, x_dt, 0.0), jnp.where(lane_lo, 0.0, x_dt)], axis=0)
                x_rem = (xh * dtrem_x[:, ps]).astype(BF16)
                inter = []
                for b, r in enumerate(seq_rows):
                    s = ssm0_ref[b, p] if single_chunk else ssmn_ref[b, p]
                    inter.append(lax.dot_general(cm_b[r], s.astype(BF16), NT_DIMS, preferred_element_type=F32))
                    keep = jnp.where(row_lo, e_last[r.start:r.start + 1, ha:ha + 1],
                                     e_last[r.start:r.start + 1, hb:hb + 1])
                    ssmn_ref[b, p] = keep * s + lax.dot_general(x_rem[r], bm_b[r], TN_DIMS,
                                                                preferred_element_type=F32)
                y = _dot(m_pair, x_bd) + cat_rows(inter) * ecum_x[:, ps] + m2_d_ref[:, ps] * xh
                z = proj_ref[:, OFF_MZ + p * LANES:OFF_MZ + (p + 1) * LANES]
                yz = y * _silu(z)
                total = total + yz * yz
                gated.append(yz)
                yield
        scale = row_rsqrt_of_squares(total, 1.0 / M2_DIM)
        for p, yz in enumerate(gated):
            ps = slice(p * LANES, (p + 1) * LANES)
            mix_ref[:, m2_off + p * LANES:m2_off + (p + 1) * LANES] = (yz * scale * m2_norm_ref[:, ps]).astype(BF16)

    _round_robin((ssd(), 1), (gla(), 1), (deltanet(), 1))

    @at_last_chunk
    def _():
        convn_ref[...] = cbuf[:, pl.ds(CONV_PAD + seq_len - CONV_TAIL, CONV_TAIL), :]

    if not single_chunk:
        cbuf[:, pl.ds(0, CONV_PAD), :] = cbuf[:, pl.ds(seq_len, CONV_PAD), :]


def _mixer(proj, row_start, nb, seq, nseq, seq_len, layer, conv0, dn0, gla0, ssm0, params,
           total_rows, prev_mix=None, prev_states=None):
    c = nseq * seq_len
    nc = seq // seq_len
    blk0 = row_start // c
    lsel = layer if conv0.shape[0] > 1 else 0
    consts, nlev, nmerge, nsquare = _mixer_constants(nseq, seq_len)
    const_args = [consts[k] for k in CONST_NAMES]

    def state_spec(shape):
        zeros = (0,) * len(shape)
        return pl.BlockSpec((None, nseq) + shape, lambda b, k: (lsel, b) + zeros)

    def out_state_spec(shape):
        zeros = (0,) * len(shape)
        return pl.BlockSpec((None, nseq) + shape, lambda b, k: (layer, b) + zeros)

    conv_shape = (CONV_TAIL, CONV_CH)
    dn_shape = (DN_HEADS, DN_DK, DN_DV)
    gla_shape = (GLA_HEADS, GLA_DK, GLA_DV)
    ssm_shape = (M2_PAIRS, 2 * M2_HEADDIM, M2_STATE)
    state_shapes = (conv_shape, dn_shape, gla_shape, ssm_shape)
    in_specs = [pl.BlockSpec((c, PROJ_COLS), lambda b, k: (blk0 + b * nc + k, 0))]
    in_specs += [state_spec(s) for s in state_shapes]
    in_specs += [_resident(p.shape) for p in params]
    in_specs += [_resident(a.shape) for a in const_args]
    args = [proj, conv0, dn0, gla0, ssm0, *params, *const_args]
    aliases = {}
    if prev_mix is not None:
        aliases[len(args)] = 0
        args.append(prev_mix)
    if prev_states is not None:
        for i, st in enumerate(prev_states):
            aliases[len(args)] = 1 + i
            args.append(st)
    n_alias = len(aliases)
    in_specs += [pl.BlockSpec(memory_space=pl.ANY)] * n_alias
    out_specs = [pl.BlockSpec((c, D_MIX), lambda b, k: (blk0 + b * nc + k, 0))]
    out_specs += [out_state_spec(s) for s in state_shapes]
    out_shape = [jax.ShapeDtypeStruct((total_rows, D_MIX), BF16)]
    out_shape += [jax.ShapeDtypeStruct((DEPTH, nb) + s, F32) for s in state_shapes]
    body = functools.partial(_mixer_kernel, nseq, seq_len, nlev, nmerge, nsquare, nc == 1, n_alias)
    return pl.pallas_call(
        body, grid=(nb // nseq, nc), in_specs=in_specs, out_specs=out_specs, out_shape=out_shape,
        input_output_aliases=aliases,
        scratch_shapes=[pltpu.VMEM((nseq, CONV_PAD + seq_len, CONV_CH), F32), pltpu.VMEM((c, CONV_CH), F32)],
        compiler_params=pltpu.CompilerParams(dimension_semantics=("parallel", "arbitrary"),
                                             vmem_limit_bytes=VMEM_LIMIT_BYTES),
        name=f"mixer_{nseq}x{seq_len}")(*args)


def _split_w_in(w):
    o_dz = CONV_CH
    o_da = o_dz + DN_V
    o_db = o_da + DN_HEADS
    o_gq = o_db + DN_HEADS
    o_glow = o_gq + 2 * GLA_QK + GLA_V
    o_gg = o_glow + GLA_RANK
    o_mdt = o_gg + GLA_V + M2_DIM
    assert o_mdt + M2_HEADS == w.shape[-1]
    wt = jnp.swapaxes(w, 1, 2)
    wide = jnp.concatenate([wt[:, :o_da], wt[:, o_gq:o_glow], wt[:, o_gg:o_mdt]], axis=1).astype(BF16)
    pad = jnp.zeros((w.shape[0], LANES - SM_USED, w.shape[1]), w.dtype)
    small = jnp.concatenate([wt[:, o_da:o_db], wt[:, o_mdt:], wt[:, o_db:o_gq], wt[:, o_glow:o_gg], pad],
                            axis=1).astype(BF16)
    return wide, small


def _small_lane_row(dn_vals, m2_vals):
    pad = jnp.zeros((LANES - SM_DB,), F32)
    return jnp.concatenate([dn_vals.astype(F32), m2_vals.astype(F32), pad]).reshape(1, LANES)


def _ssm_to_pairs(s):
    t = jnp.swapaxes(s, -1, -2)
    return t.reshape(t.shape[:-3] + (M2_PAIRS, 2 * M2_HEADDIM, M2_STATE))


def _ssm_from_pairs(s):
    t = s.reshape(s.shape[:-3] + (M2_HEADS, M2_HEADDIM, M2_STATE))
    return jnp.swapaxes(t, -1, -2)


def kernel(x_prompt, x_sample, state_conv, state_delta, state_gla, state_ssm, ffn1_norm, ffn1_w_gu, ffn1_w_down, mix_norm, w_in, conv_w, conv_b, dn_a_log, dn_dt_bias, dn_norm, gla_w_up, gla_b_up, gla_norm, m2_a_log, m2_dt_bias, m2_d, m2_norm, w_out, ffn2_norm, ffn2_w_gu, ffn2_w_down, final_norm):
    bp, sp, _ = x_prompt.shape
    bs, ss, _ = x_sample.shape
    tp, ts = bp * sp, bs * ss
    assert tp % DENSE_TILE == 0 and ts % DENSE_TILE == 0 and sp % PROMPT_CHUNK == 0
    assert bs % SAMPLE_SEQS_PER_STEP == 0 and tp % (SAMPLE_SEQS_PER_STEP * ss) == 0 and ss % SUBLANES == 0

    zero_conv = jnp.zeros((1, bp, CONV_TAIL, CONV_CH), F32)
    zero_dn = jnp.zeros((1, bp, DN_HEADS, DN_DK, DN_DV), F32)
    zero_gla = jnp.zeros((1, bp, GLA_HEADS, GLA_DK, GLA_DV), F32)
    zero_ssm = jnp.zeros((1, bp, M2_PAIRS, 2 * M2_HEADDIM, M2_STATE), F32)
    ssm_pairs = _ssm_to_pairs(state_ssm)

    wgu1, wd1, wgu2, wd2, wout = ffn1_w_gu, ffn1_w_down, ffn2_w_gu, ffn2_w_down, w_out
    win_wide, win_small = _split_w_in(w_in)
    g_ffn1, g_mix, g_ffn2 = (g.reshape(DEPTH, 1, D_MODEL) for g in (ffn1_norm, mix_norm, ffn2_norm))

    xs = [x_prompt.reshape(tp, D_MODEL), x_sample.reshape(ts, D_MODEL)]
    st_p = st_s = None
    for l in range(DEPTH):
        params = (conv_w[l], conv_b[l].reshape(1, CONV_CH),
                  _small_lane_row(dn_a_log[l], m2_a_log[l]), _small_lane_row(dn_dt_bias[l], m2_dt_bias[l]),
                  dn_norm[l].reshape(1, DN_DV),
                  gla_w_up[l].astype(BF16), gla_b_up[l].reshape(1, GLA_QK), gla_norm[l].reshape(1, GLA_DV),
                  jnp.repeat(m2_d[l], M2_HEADDIM).reshape(1, M2_DIM), m2_norm[l].reshape(1, M2_DIM))

        x = _ffn(xs, l, g_ffn1, wgu1, wd1)
        proj = _inproj(x, l, g_mix, win_wide, win_small)
        mix, *st_p = _mixer(proj, 0, bp, sp, 1, PROMPT_CHUNK, l, zero_conv, zero_dn, zero_gla, zero_ssm, params,
                            tp + ts, prev_states=st_p)
        mix, *st_s = _mixer(proj, tp, bs, ss, SAMPLE_SEQS_PER_STEP, ss, l,
                            state_conv, state_delta, state_gla, ssm_pairs, params,
                            tp + ts, prev_mix=mix, prev_states=st_s)
        if l < DEPTH - 1:
            xs = [_ffn([x], l, g_ffn2, wgu2, wd2, mix=mix, wout=wout)]
        else:
            y_prompt, y_sample = _ffn([x], l, g_ffn2, wgu2, wd2, final_gain=final_norm, split_rows=tp,
                                      mix=mix, wout=wout)

    return (y_prompt.reshape(bp, sp, D_MODEL), y_sample.reshape(bs, ss, D_MODEL),
            st_p[0], st_p[1], st_p[2], _ssm_from_pairs(st_p[3]),
            st_s[0], st_s[1], st_s[2], _ssm_from_pairs(st_s[3]))
```

```python
import functools
import math

import jax
import jax.numpy as jnp
import numpy as np
from jax import lax
from jax.experimental import pallas as pl
from jax.experimental.pallas import tpu as pltpu

F32 = jnp.float32
BF16 = jnp.bfloat16
LOG2E = math.log2(math.e)

D_MODEL = 1024
DEPTH = 2
D_MIX = 2 * D_MODEL
D_FF = 2816
CONV_W = 4
DN_HEADS = 4
DN_DK = 128
DN_DV = 128
GLA_HEADS = 4
GLA_DK = 64
GLA_DV = 128
GLA_RANK = 16
GLA_NORMALIZER = 16.0
M2_DIM = 1024
M2_HEADDIM = 64
M2_HEADS = 16
M2_STATE = 128
M2_GROUPS = 2
M2_PAIRS = M2_HEADS // 2
M2_PAIRS_PER_GROUP = M2_PAIRS // M2_GROUPS
EPS = 1e-6
LANES = 128
SUBLANES = 8

DN_QK = DN_HEADS * DN_DK
DN_V = DN_HEADS * DN_DV
GLA_QK = GLA_HEADS * GLA_DK
GLA_V = GLA_HEADS * GLA_DV
M2_BC = M2_GROUPS * M2_STATE
CONV_CH = 3 * 512 + M2_DIM + 2 * M2_BC

OFF_DQ = 0
OFF_DK = OFF_DQ + DN_QK
OFF_DV = OFF_DK + DN_QK
OFF_MX = OFF_DV + DN_V
OFF_MB = OFF_MX + M2_DIM
OFF_MC = OFF_MB + M2_BC
OFF_DZ = CONV_CH
OFF_GQ = OFF_DZ + DN_V
OFF_GK = OFF_GQ + GLA_QK
OFF_GV = OFF_GK + GLA_QK
OFF_GG = OFF_GV + GLA_V
OFF_MZ = OFF_GG + GLA_V
OFF_SMALL = OFF_MZ + M2_DIM
SM_DA = 0
SM_MDT = SM_DA + DN_HEADS
SM_DB = SM_MDT + M2_HEADS
SM_GLOW = SM_DB + DN_HEADS
SM_USED = SM_GLOW + GLA_RANK
PROJ_COLS = OFF_SMALL + LANES

CONV_PAD = SUBLANES
CONV_TAIL = CONV_W - 1
NEUMANN_BLOCK = 16
VMEM_LIMIT_BYTES = 56 * 1024 * 1024
PROMPT_CHUNK = 128
SAMPLE_SEQS_PER_STEP = 8
DENSE_TILE = 256
WEIGHT_CHUNK = 512
WEIGHT_ROWS = 256
TRANSPOSE_ROWS = 768
MASKED = -1e30
NT_DIMS = (((1,), (1,)), ((), ()))
TN_DIMS = (((0,), (0,)), ((), ()))


def _dot(a, b):
    return jnp.dot(a.astype(BF16), b.astype(BF16), preferred_element_type=F32)


def _dot_nt(a, b):
    return lax.dot_general(a.astype(BF16), b.astype(BF16), NT_DIMS, preferred_element_type=F32)


def _dot_tn(a, b):
    return lax.dot_general(a.astype(BF16), b.astype(BF16), TN_DIMS, preferred_element_type=F32)


def _split2(x):
    hi = x.astype(BF16)
    return hi, (x - hi.astype(F32)).astype(BF16)


def _split3(x):
    hi = x.astype(BF16)
    r1 = x - hi.astype(F32)
    mid = r1.astype(BF16)
    lo = (r1 - mid.astype(F32)).astype(BF16)
    return hi, mid, lo


def _sum_dots(pieces, fn):
    acc = None
    for piece in pieces:
        part = fn(piece)
        acc = part if acc is None else acc + part
    return acc


def _dot_exact_lhs(m, pieces):
    return _sum_dots(pieces, lambda p: jnp.dot(m, p, preferred_element_type=F32))


def _dot_exact_rhs(pieces, m):
    return _sum_dots(pieces, lambda p: jnp.dot(p, m, preferred_element_type=F32))


def _dot_exact_tn(pieces, m):
    return _sum_dots(pieces, lambda p: lax.dot_general(p, m, TN_DIMS, preferred_element_type=F32))


def _silu(x):
    return x * jax.nn.sigmoid(x)


def _softplus(x):
    return jnp.maximum(x, 0.0) + jnp.log(1.0 + jnp.exp(-jnp.abs(x)))


def _rms(x, gain):
    return x * lax.rsqrt(jnp.mean(x * x, axis=-1, keepdims=True) + EPS) * gain


def _stream_cast(chunks, stage, sem):
    def copy(j):
        return pltpu.make_async_copy(chunks[j][0], stage.at[j % 2], sem.at[j % 2])

    copy(0).start()
    for j, (_, dst) in enumerate(chunks):
        if j + 1 < len(chunks):
            copy(j + 1).start()
        copy(j).wait()
        dst[...] = stage[j % 2].astype(BF16)


def _ffn_kernel(layer, split_tile, final_norm, with_outproj, *refs):
    refs = list(refs)
    two_in = split_tile is not None and not final_norm
    two_out = split_tile is not None and final_norm
    n_out = 2 if two_out else 1
    x_refs = [refs.pop(0) for _ in range(2 if two_in else 1)]
    mix_ref, wout_hbm = (refs.pop(0), refs.pop(0)) if with_outproj else (None, None)
    g_ref, wgu_hbm, wd_hbm = (refs.pop(0) for _ in range(3))
    fg_ref = refs.pop(0) if final_norm else None
    o_refs = [refs.pop(0) for _ in range(n_out)]
    wgu_ref, wd_ref = refs.pop(0), refs.pop(0)
    wout_ref = refs.pop(0) if with_outproj else None
    stage_cols, stage_rows, sem = refs
    i = pl.program_id(0)

    @pl.when(i == 0)
    def _():
        cols = [(wgu_hbm.at[layer, :, pl.ds(j * WEIGHT_CHUNK, WEIGHT_CHUNK)],
                 wgu_ref.at[:, pl.ds(j * WEIGHT_CHUNK, WEIGHT_CHUNK)]) for j in range(2 * D_FF // WEIGHT_CHUNK)]
        _stream_cast(cols, stage_cols, sem)
        rows = [(wd_hbm.at[layer, pl.ds(j * WEIGHT_ROWS, WEIGHT_ROWS), :],
                 wd_ref.at[pl.ds(j * WEIGHT_ROWS, WEIGHT_ROWS), :]) for j in range(D_FF // WEIGHT_ROWS)]
        if with_outproj:
            rows += [(wout_hbm.at[layer, pl.ds(j * WEIGHT_ROWS, WEIGHT_ROWS), :],
                      wout_ref.at[pl.ds(j * WEIGHT_ROWS, WEIGHT_ROWS), :]) for j in range(D_MIX // WEIGHT_ROWS)]
        _stream_cast(rows, stage_rows, sem)

    x = jnp.where(i < split_tile, x_refs[0][...], x_refs[1][...]) if two_in else x_refs[0][...]
    if with_outproj:
        x = x + jnp.dot(mix_ref[...], wout_ref[...], preferred_element_type=F32)
    xn = _rms(x, g_ref[...]).astype(BF16)
    gate = jnp.dot(xn, wgu_ref[:, :D_FF], preferred_element_type=F32)
    up = jnp.dot(xn, wgu_ref[:, D_FF:], preferred_element_type=F32)
    h = (_silu(gate) * up).astype(BF16)
    y = x + 0.5 * jnp.dot(h, wd_ref[...], preferred_element_type=F32)
    if final_norm:
        y = _rms(y, fg_ref[...])
    if two_out:
        @pl.when(i < split_tile)
        def _():
            o_refs[0][...] = y

        @pl.when(i >= split_tile)
        def _():
            o_refs[1][...] = y
    else:
        o_refs[0][...] = y


def _inproj_kernel(x_ref, g_ref, wt_ref, wsmall_ref, o_ref, w_ref):
    @pl.when(pl.program_id(0) == 0)
    def _():
        for r in range(0, OFF_SMALL, TRANSPOSE_ROWS):
            w_ref[:, r:r + TRANSPOSE_ROWS] = wt_ref[r:r + TRANSPOSE_ROWS, :].T
        w_ref[:, OFF_SMALL:PROJ_COLS] = wsmall_ref[...].T

    xn = _rms(x_ref[...], g_ref[...]).astype(BF16)
    o_ref[...] = jnp.dot(xn, w_ref[...], preferred_element_type=F32)


def _resident(shape):
    nd = len(shape)
    return pl.BlockSpec(shape, lambda *_: (0,) * nd, pipeline_mode=pl.Buffered(1))


def _layer_resident(arr, layer):
    zeros = (0,) * (arr.ndim - 1)
    return pl.BlockSpec((None,) + arr.shape[1:], lambda *_: (layer,) + zeros, pipeline_mode=pl.Buffered(1))


def _ffn(xs, layer, gains, wgu, wd, final_gain=None, split_rows=None, mix=None, wout=None):
    assert (2 * D_FF) % WEIGHT_CHUNK == 0 and D_FF % WEIGHT_ROWS == 0 and D_MIX % WEIGHT_ROWS == 0
    t = sum(x.shape[0] for x in xs)
    two_in = len(xs) == 2
    two_out = final_gain is not None and split_rows is not None
    split_tile = None
    if two_in or two_out:
        split_tile = (xs[0].shape[0] if two_in else split_rows) // DENSE_TILE
    tile = pl.BlockSpec((DENSE_TILE, D_MODEL), lambda i: (i, 0))
    first = pl.BlockSpec((DENSE_TILE, D_MODEL), lambda i: (jnp.minimum(i, split_tile - 1), 0))
    second = pl.BlockSpec((DENSE_TILE, D_MODEL), lambda i: (jnp.maximum(i - split_tile, 0), 0))
    in_specs = [first, second] if two_in else [tile]
    args = list(xs)
    hbm = pl.BlockSpec(memory_space=pl.ANY)
    scratch = [pltpu.VMEM((D_MODEL, 2 * D_FF), BF16), pltpu.VMEM((D_FF, D_MODEL), BF16)]
    if mix is not None:
        in_specs += [pl.BlockSpec((DENSE_TILE, D_MIX), lambda i: (i, 0)), hbm]
        args += [mix, wout]
        scratch.append(pltpu.VMEM((D_MIX, D_MODEL), BF16))
    scratch += [pltpu.VMEM((2, D_MODEL, WEIGHT_CHUNK), F32), pltpu.VMEM((2, WEIGHT_ROWS, D_MODEL), F32),
                pltpu.SemaphoreType.DMA((2,))]
    in_specs += [_layer_resident(gains, layer), hbm, hbm]
    args += [gains, wgu, wd]
    if final_gain is not None:
        in_specs.append(_resident((1, D_MODEL)))
        args.append(final_gain.reshape(1, D_MODEL))
    if two_out:
        out_specs = [first, second]
        out_shape = [jax.ShapeDtypeStruct((split_rows, D_MODEL), F32),
                     jax.ShapeDtypeStruct((t - split_rows, D_MODEL), F32)]
    else:
        out_specs, out_shape = tile, jax.ShapeDtypeStruct((t, D_MODEL), F32)
    return pl.pallas_call(
        functools.partial(_ffn_kernel, layer, split_tile, final_gain is not None, mix is not None),
        grid=(t // DENSE_TILE,), in_specs=in_specs, out_specs=out_specs, out_shape=out_shape,
        scratch_shapes=scratch,
        compiler_params=pltpu.CompilerParams(dimension_semantics=("arbitrary",),
                                             vmem_limit_bytes=VMEM_LIMIT_BYTES),
        name="ffn" if final_gain is None else "ffn_final_norm")(*args)


def _inproj(x, layer, gains, wt, wsmall):
    t = x.shape[0]
    assert OFF_SMALL % TRANSPOSE_ROWS == 0
    return pl.pallas_call(
        _inproj_kernel, grid=(t // DENSE_TILE,),
        in_specs=[pl.BlockSpec((DENSE_TILE, D_MODEL), lambda i: (i, 0)), _layer_resident(gains, layer),
                  _layer_resident(wt, layer), _layer_resident(wsmall, layer)],
        out_specs=pl.BlockSpec((DENSE_TILE, PROJ_COLS), lambda i: (i, 0)),
        out_shape=jax.ShapeDtypeStruct((t, PROJ_COLS), F32),
        scratch_shapes=[pltpu.VMEM((D_MODEL, PROJ_COLS), BF16)],
        compiler_params=pltpu.CompilerParams(dimension_semantics=("arbitrary",), vmem_limit_bytes=VMEM_LIMIT_BYTES),
        name="inproj")(x, gains, wt, wsmall)


CONST_NAMES = ("tril", "negi", "blk", "merge", "sel", "lv", "expand", "bc_dn", "bc_beta", "bc_ssd")


def _mixer_constants(nseq, seq_len):
    c = nseq * seq_len
    idx = np.arange(c)
    seq, pos = idx // seq_len, idx % seq_len
    same = seq[:, None] == seq[None, :]
    pi, pj = pos[:, None], pos[None, :]
    causal = same & (pi >= pj)
    base = min(seq_len, NEUMANN_BLOCK)
    merges = []
    s = base
    while s < seq_len:
        merges.append(same & (pi // (2 * s) == pj // (2 * s)) & (pi // s != pj // s))
        s *= 2
    levels = []
    m = seq_len // 2
    while m >= 1:
        levels.append(m)
        m //= 2
    n_coarse = sum(1 for m in levels if m >= SUBLANES)
    sel = np.zeros((len(levels), c, c), np.float32)
    lv = np.full((c, c), -1, np.int32)
    xor = pi ^ pj
    for k, m in enumerate(levels):
        mid = seq * seq_len + (pos // (2 * m)) * (2 * m) + (m - 1)
        sel[k, idx, mid] = 1.0
        lv[same & (pi > pj) & (xor >= m) & (xor < 2 * m)] = k
    lv[idx, idx] = len(levels)
    expand = np.zeros((LANES, M2_DIM), np.float32)
    bc_ssd = np.zeros((LANES, M2_HEADS * LANES), np.float32)
    for h in range(M2_HEADS):
        expand[SM_MDT + h, h * M2_HEADDIM:(h + 1) * M2_HEADDIM] = 1.0
        bc_ssd[SM_MDT + h, h * LANES:(h + 1) * LANES] = 1.0
    bc_dn = np.zeros((LANES, DN_HEADS * LANES), np.float32)
    bc_beta = np.zeros((LANES, DN_HEADS * LANES), np.float32)
    for h in range(DN_HEADS):
        bc_dn[SM_DA + h, h * LANES:(h + 1) * LANES] = 1.0
        bc_beta[SM_DB + h, h * LANES:(h + 1) * LANES] = 1.0
    consts = dict(
        tril=jnp.asarray(causal, BF16),
        negi=jnp.asarray(np.where(causal, 0.0, MASKED), F32),
        blk=jnp.asarray(same & (pi // base == pj // base), F32),
        merge=jnp.asarray(np.stack(merges) if merges else np.zeros((1, SUBLANES, LANES)), F32),
        sel=jnp.asarray(sel[n_coarse:].reshape((len(levels) - n_coarse) * c, c), BF16),
        lv=jnp.asarray(lv),
        expand=jnp.asarray(expand, BF16),
        bc_dn=jnp.asarray(bc_dn, BF16),
        bc_beta=jnp.asarray(bc_beta, BF16),
        bc_ssd=jnp.asarray(bc_ssd, BF16),
    )
    nsquare = int(math.log2(base)) - 1
    return consts, len(levels), n_coarse, len(merges), nsquare


def _round_robin(*staged):
    pending = list(staged)
    while pending:
        for item in list(pending):
            g, per_turn = item
            try:
                for _ in range(per_turn):
                    next(g)
            except StopIteration:
                pending.remove(item)


def _mixer_kernel(nseq, seq_len, nlev, n_coarse, nmerge, nsquare, single_chunk, n_alias,
                  proj_ref, conv0_ref, dn0_ref, gla0_ref, ssm0_ref,
                  convw_ref, convb_ref, alog_ref, dtb_ref, dn_norm_ref, gla_wup_ref, gla_bup_ref, gla_norm_ref,
                  m2_d_ref, m2_norm_ref,
                  tril_ref, negi_ref, blk_ref, merge_ref, sel_ref, lv_ref, expand_ref, bc_dn_ref, bc_beta_ref,
                  bc_ssd_ref, *rest):
    mix_ref, convn_ref, dnn_ref, glan_ref, ssmn_ref, cbuf, cvout = rest[n_alias:]
    c = nseq * seq_len
    chunk = pl.program_id(1)
    last_chunk = pl.num_programs(1) - 1
    seq_rows = [slice(b * seq_len, (b + 1) * seq_len) for b in range(nseq)]

    def at_first_chunk(fn):
        if single_chunk:
            fn()
        else:
            pl.when(chunk == 0)(fn)

    def at_last_chunk(fn):
        if single_chunk:
            fn()
        else:
            pl.when(chunk == last_chunk)(fn)

    @at_first_chunk
    def _():
        cbuf[:, pl.ds(CONV_PAD - CONV_TAIL, CONV_TAIL), :] = conv0_ref[...]
        if not single_chunk:
            dnn_ref[...] = dn0_ref[...]
            glan_ref[...] = gla0_ref[...]
            ssmn_ref[...] = ssm0_ref[...]

    cbuf[:, pl.ds(CONV_PAD, seq_len), :] = proj_ref[:, 0:CONV_CH].reshape(nseq, seq_len, CONV_CH)

    def conv(lo, width):
        acc = convb_ref[:, lo:lo + width].reshape(1, 1, width)
        for w in range(CONV_W):
            tap = convw_ref[w:w + 1, lo:lo + width].reshape(1, 1, width)
            acc = acc + tap * cbuf[:, pl.ds(CONV_PAD - CONV_TAIL + w, seq_len), lo:lo + width]
        return _silu(acc).reshape(c, width)

    for lo in range(0, CONV_CH, 512):
        cvout[:, lo:lo + 512] = conv(lo, 512)
    dq_all = cvout[:, OFF_DQ:OFF_DQ + DN_QK]
    dk_all = cvout[:, OFF_DK:OFF_DK + DN_QK]
    dv_all = cvout[:, OFF_DV:OFF_DV + DN_V]
    mx_all = [cvout[:, OFF_MX + p * LANES:OFF_MX + (p + 1) * LANES] for p in range(M2_PAIRS)]
    mb_all = [cvout[:, OFF_MB + g * M2_STATE:OFF_MB + (g + 1) * M2_STATE] for g in range(M2_GROUPS)]
    mc_all = [cvout[:, OFF_MC + g * M2_STATE:OFF_MC + (g + 1) * M2_STATE] for g in range(M2_GROUPS)]

    ones_sq = jnp.ones((LANES, LANES), BF16)

    def row_rsqrt(x, mult):
        return row_rsqrt_of_squares(x * x, mult)

    def row_rsqrt_of_squares(sq, mult):
        sums = jnp.dot(sq.astype(BF16), ones_sq, preferred_element_type=F32)
        return lax.rsqrt(sums * mult + EPS)

    def per_seq_last(x):
        parts = [jnp.broadcast_to(x[r.stop - 1:r.stop, :], (seq_len, x.shape[1])) for r in seq_rows]
        return parts[0] if nseq == 1 else jnp.concatenate(parts, axis=0)

    def cat_rows(parts):
        return parts[0] if nseq == 1 else jnp.concatenate(parts, axis=0)

    tril = tril_ref[...]
    negi = negi_ref[...]
    lv = lv_ref[...]
    diag = lv == nlev

    small = proj_ref[:, OFF_SMALL:OFF_SMALL + LANES]
    sp = _softplus(small + dtb_ref[...])
    g2 = (-LOG2E * jnp.exp(alog_ref[...])) * sp
    cum = _dot_exact_lhs(tril, _split3(g2))
    cum_t = cum.T
    last = per_seq_last(cum)
    e_cum = jnp.exp2(cum)
    e_rem = jnp.exp2(last - cum)
    e_last = jnp.exp2(last)
    cum_p, e_cum_p, e_rem_p = _split3(cum), _split2(e_cum), _split2(e_rem)

    def deltanet():
        bc_dn = bc_dn_ref[...]
        cum_col = _dot_exact_rhs(cum_p, bc_dn)
        egc_col = _dot_exact_rhs(e_cum_p, bc_dn)
        erem_col = _dot_exact_rhs(e_rem_p, bc_dn)
        beta_col = _dot_exact_rhs(_split2(jax.nn.sigmoid(small)), bc_beta_ref[...])
        yield
        dn = []
        for h in range(DN_HEADS):
            hs = slice(h * DN_DK, (h + 1) * DN_DK)
            dq, dk = dq_all[:, hs], dk_all[:, hs]
            st = dict(q=dq * (row_rsqrt(dq, 1.0) * (DN_DK ** -0.5)), k=dk * row_rsqrt(dk, 1.0),
                      v=dv_all[:, hs], beta=beta_col[:, hs], egc=egc_col[:, hs], erem=erem_col[:, hs])
            dn.append(st)
        yield
        for h, st in enumerate(dn):
            kb, qb = st["k"].astype(BF16), st["q"].astype(BF16)
            decay = jnp.exp2(cum_col[:, h * LANES:h * LANES + c] - cum_t[SM_DA + h:SM_DA + h + 1, :] + negi)
            kk = lax.dot_general(kb, kb, NT_DIMS, preferred_element_type=F32)
            qk = lax.dot_general(qb, kb, NT_DIMS, preferred_element_type=F32)
            st["a"] = jnp.where(diag, 0.0, kk * decay * st["beta"][:, :c])
            st["qkd"] = qk * decay
        yield
        blk = blk_ref[...]
        for st in dn:
            st["p"] = st["a"] * blk
            st["n"] = -st["p"]
        for _ in range(nsquare):
            for st in dn:
                st["p"] = _dot(st["p"], st["p"])
            yield
            for st in dn:
                st["n"] = st["n"] + st["p"] + _dot(st["n"], st["p"])
            yield
        for lvl in range(nmerge):
            emask = merge_ref[lvl]
            for st in dn:
                e = st["a"] * emask
                st["y"] = e + _dot(e, st["n"])
            yield
            for st in dn:
                st["n"] = st["n"] - st["y"] - _dot(st["n"], st["y"])
            yield
        for st in dn:
            rhs = jnp.concatenate([st["k"] * (st["beta"] * st["egc"]), st["v"] * st["beta"]], axis=1)
            st["sol"] = rhs + _dot(st["n"], rhs)
        yield
        for h, st in enumerate(dn):
            sol = st["sol"]
            w_mat, u_mat = sol[:, :DN_DK], sol[:, DN_DK:]
            qe = st["q"] * st["egc"]
            v_parts, o_parts, states = [], [], []
            for b, r in enumerate(seq_rows):
                s = dn0_ref[b, h] if single_chunk else dnn_ref[b, h]
                both = _dot(jnp.concatenate([w_mat[r], qe[r]], axis=0), s)
                v_parts.append(u_mat[r] - both[:seq_len])
                o_parts.append(both[seq_len:])
                states.append(s)
            st["v_new"], st["o"], st["s"] = cat_rows(v_parts), cat_rows(o_parts), states
        yield
        for h, st in enumerate(dn):
            v_new = st["v_new"]
            o = st["o"] + _dot(st["qkd"], v_new)
            kdec = st["k"] * st["erem"]
            lane = SM_DA + h
            for b, r in enumerate(seq_rows):
                dnn_ref[b, h] = (e_last[r.start:r.start + 1, lane:lane + 1] * st["s"][b]
                                 + _dot_tn(kdec[r], v_new[r]))
            st["o"] = o
        yield
        for h, st in enumerate(dn):
            o = st["o"]
            z = proj_ref[:, OFF_DZ + h * DN_DV:OFF_DZ + (h + 1) * DN_DV]
            o_n = o * row_rsqrt(o, 1.0 / DN_DV) * dn_norm_ref[...]
            mix_ref[:, h * DN_DV:(h + 1) * DN_DV] = (o_n * _silu(z)).astype(BF16)

    def gla():
        glow = small[:, SM_GLOW:SM_GLOW + GLA_RANK]
        zg = _dot(glow, gla_wup_ref[...]) + gla_bup_ref[...]
        gk2 = _softplus(-zg) * (-LOG2E / GLA_NORMALIZER)
        gcum = _dot_exact_lhs(tril, _split3(gk2))
        yield
        gq = proj_ref[:, OFF_GQ:OFF_GQ + GLA_QK] * (GLA_DK ** -0.5)
        gkk = proj_ref[:, OFF_GK:OFF_GK + GLA_QK]
        gq_b, gkk_b = gq.astype(BF16), gkk.astype(BF16)
        gcum_p = _split3(gcum)
        gmid_fine = _dot_exact_lhs(sel_ref[...], gcum_p)

        def gmid(lev):
            if lev >= n_coarse:
                return gmid_fine[(lev - n_coarse) * c:(lev - n_coarse + 1) * c, :]
            m = seq_len >> (lev + 1)
            rows = [jnp.broadcast_to(gcum[start + m - 1:start + m, :], (2 * m, GLA_QK))
                    for start in range(0, c, 2 * m)]
            return rows[0] if len(rows) == 1 else jnp.concatenate(rows, axis=0)

        yield
        level_ops = []
        for lev in range(nlev):
            e = jnp.exp2(-jnp.abs(gcum - gmid(lev)))
            e_b = e.astype(BF16)
            level_ops.append((gq_b * e_b, gkk_b * e_b, lv == lev))
            if lev % 2 == 1:
                yield
        glast = per_seq_last(gcum)
        q_inter = gq * jnp.exp2(gcum)
        k_dec = gkk * jnp.exp2(glast - gcum)
        eye_dk = (lax.broadcasted_iota(jnp.int32, (GLA_DK, GLA_DK), 0)
                  == lax.broadcasted_iota(jnp.int32, (GLA_DK, GLA_DK), 1))
        ones_dk = jnp.ones((GLA_DK, GLA_DV), BF16)
        yield
        for h in range(GLA_HEADS):
            hs = slice(h * GLA_DK, (h + 1) * GLA_DK)
            v = proj_ref[:, OFF_GV + h * GLA_DV:OFF_GV + (h + 1) * GLA_DV]
            scores = jnp.where(diag, lax.dot_general(gq_b[:, hs], gkk_b[:, hs], NT_DIMS,
                                                     preferred_element_type=F32), 0.0)
            for q_t, k_t, mask in level_ops:
                s_lev = lax.dot_general(q_t[:, hs], k_t[:, hs], NT_DIMS, preferred_element_type=F32)
                scores = jnp.where(mask, s_lev, scores)
            yield
            o_parts = []
            for b, r in enumerate(seq_rows):
                s = gla0_ref[b, h] if single_chunk else glan_ref[b, h]
                o_parts.append(_dot(q_inter[r, hs], s))
                lam = jnp.where(eye_dk, jnp.exp2(glast[r.start:r.start + 1, hs]), 0.0)
                keep = _dot_exact_rhs(_split3(lam), ones_dk)
                glan_ref[b, h] = keep * s + _dot_tn(k_dec[r, hs], v[r])
            o = cat_rows(o_parts) + _dot(scores, v)
            yield
            z = proj_ref[:, OFF_GG + h * GLA_DV:OFF_GG + (h + 1) * GLA_DV]
            o_n = o * row_rsqrt(o, 1.0 / GLA_DV) * gla_norm_ref[...]
            mix_ref[:, DN_V + h * GLA_DV:DN_V + (h + 1) * GLA_DV] = (o_n * _silu(z)).astype(BF16)

    def ssd():
        expand = expand_ref[...]
        sp_p = _split2(sp)
        dt_x = _dot_exact_rhs(sp_p, expand)
        ecum_x = _dot_exact_rhs(e_cum_p, expand)
        dtrem_x = _dot_exact_rhs(_split2(sp * e_rem), expand)
        yield
        cum_cols = _dot_exact_rhs(cum_p, bc_ssd_ref[...])
        lane_lo = lax.broadcasted_iota(jnp.int32, (c, LANES), 1) < M2_HEADDIM
        row_lo = lax.broadcasted_iota(jnp.int32, (LANES, LANES), 0) < M2_HEADDIM
        yield
        m2_off = DN_V + GLA_V
        total = jnp.zeros((c, LANES), F32)
        gated = []
        for g in range(M2_GROUPS):
            bm, cm = mb_all[g], mc_all[g]
            bm_b, cm_b = bm.astype(BF16), cm.astype(BF16)
            cb = lax.dot_general(cm_b, bm_b, NT_DIMS, preferred_element_type=F32)
            yield
            for pp in range(M2_PAIRS_PER_GROUP):
                p = g * M2_PAIRS_PER_GROUP + pp
                ps = slice(p * LANES, (p + 1) * LANES)
                ha, hb = SM_MDT + 2 * p, SM_MDT + 2 * p + 1
                xh = mx_all[p]
                x_dt = xh * dt_x[:, ps]
                col_a, col_b = 2 * p * LANES, (2 * p + 1) * LANES
                dec_a = jnp.exp2(cum_cols[:, col_a:col_a + c] - cum_t[ha:ha + 1, :] + negi)
                dec_b = jnp.exp2(cum_cols[:, col_b:col_b + c] - cum_t[hb:hb + 1, :] + negi)
                m_pair = jnp.concatenate([cb * dec_a, cb * dec_b], axis=1)
                x_bd = jnp.concatenate([jnp.where(lane_lo, x_dt, 0.0), jnp.where(lane_lo, 0.0, x_dt)], axis=0)
                x_rem = (xh * dtrem_x[:, ps]).astype(BF16)
                inter = []
                for b, r in enumerate(seq_rows):
                    s = ssm0_ref[b, p] if single_chunk else ssmn_ref[b, p]
                    inter.append(lax.dot_general(cm_b[r], s.astype(BF16), NT_DIMS, preferred_element_type=F32))
                    keep = jnp.where(row_lo, e_last[r.start:r.start + 1, ha:ha + 1],
                                     e_last[r.start:r.start + 1, hb:hb + 1])
                    ssmn_ref[b, p] = keep * s + lax.dot_general(x_rem[r], bm_b[r], TN_DIMS,
                                                                preferred_element_type=F32)
                y = _dot(m_pair, x_bd) + cat_rows(inter) * ecum_x[:, ps] + m2_d_ref[:, ps] * xh
                z = proj_ref[:, OFF_MZ + p * LANES:OFF_MZ + (p + 1) * LANES]
                yz = y * _silu(z)
                total = total + yz * yz
                gated.append(yz)
                yield
        scale = row_rsqrt_of_squares(total, 1.0 / M2_DIM)
        for p, yz in enumerate(gated):
            ps = slice(p * LANES, (p + 1) * LANES)
            mix_ref[:, m2_off + p * LANES:m2_off + (p + 1) * LANES] = (yz * scale * m2_norm_ref[:, ps]).astype(BF16)

    _round_robin((deltanet(), 1), (ssd(), 1), (gla(), 1))

    @at_last_chunk
    def _():
        convn_ref[...] = cbuf[:, pl.ds(CONV_PAD + seq_len - CONV_TAIL, CONV_TAIL), :]

    if not single_chunk:
        cbuf[:, pl.ds(0, CONV_PAD), :] = cbuf[:, pl.ds(seq_len, CONV_PAD), :]


def _mixer(proj, row_start, nb, seq, nseq, seq_len, layer, conv0, dn0, gla0, ssm0, params,
           total_rows, prev_mix=None, prev_states=None):
    c = nseq * seq_len
    nc = seq // seq_len
    blk0 = row_start // c
    lsel = layer if conv0.shape[0] > 1 else 0
    consts, nlev, n_coarse, nmerge, nsquare = _mixer_constants(nseq, seq_len)
    const_args = [consts[k] for k in CONST_NAMES]

    def state_spec(shape):
        zeros = (0,) * len(shape)
        return pl.BlockSpec((None, nseq) + shape, lambda b, k: (lsel, b) + zeros)

    def out_state_spec(shape):
        zeros = (0,) * len(shape)
        return pl.BlockSpec((None, nseq) + shape, lambda b, k: (layer, b) + zeros)

    conv_shape = (CONV_TAIL, CONV_CH)
    dn_shape = (DN_HEADS, DN_DK, DN_DV)
    gla_shape = (GLA_HEADS, GLA_DK, GLA_DV)
    ssm_shape = (M2_PAIRS, 2 * M2_HEADDIM, M2_STATE)
    state_shapes = (conv_shape, dn_shape, gla_shape, ssm_shape)
    in_specs = [pl.BlockSpec((c, PROJ_COLS), lambda b, k: (blk0 + b * nc + k, 0))]
    in_specs += [state_spec(s) for s in state_shapes]
    in_specs += [_resident(p.shape) for p in params]
    in_specs += [_resident(a.shape) for a in const_args]
    args = [proj, conv0, dn0, gla0, ssm0, *params, *const_args]
    aliases = {}
    if prev_mix is not None:
        aliases[len(args)] = 0
        args.append(prev_mix)
    if prev_states is not None:
        for i, st in enumerate(prev_states):
            aliases[len(args)] = 1 + i
            args.append(st)
    n_alias = len(aliases)
    in_specs += [pl.BlockSpec(memory_space=pl.ANY)] * n_alias
    out_specs = [pl.BlockSpec((c, D_MIX), lambda b, k: (blk0 + b * nc + k, 0))]
    out_specs += [out_state_spec(s) for s in state_shapes]
    out_shape = [jax.ShapeDtypeStruct((total_rows, D_MIX), BF16)]
    out_shape += [jax.ShapeDtypeStruct((DEPTH, nb) + s, F32) for s in state_shapes]
    body = functools.partial(_mixer_kernel, nseq, seq_len, nlev, n_coarse, nmerge, nsquare, nc == 1, n_alias)
    return pl.pallas_call(
        body, grid=(nb // nseq, nc), in_specs=in_specs, out_specs=out_specs, out_shape=out_shape,
        input_output_aliases=aliases,
        scratch_shapes=[pltpu.VMEM((nseq, CONV_PAD + seq_len, CONV_CH), F32), pltpu.VMEM((c, CONV_CH), F32)],
        compiler_params=pltpu.CompilerParams(dimension_semantics=("parallel", "arbitrary"),
                                             vmem_limit_bytes=VMEM_LIMIT_BYTES),
        name=f"mixer_{nseq}x{seq_len}")(*args)


def _split_w_in(w):
    o_dz = CONV_CH
    o_da = o_dz + DN_V
    o_db = o_da + DN_HEADS
    o_gq = o_db + DN_HEADS
    o_glow = o_gq + 2 * GLA_QK + GLA_V
    o_gg = o_glow + GLA_RANK
    o_mdt = o_gg + GLA_V + M2_DIM
    assert o_mdt + M2_HEADS == w.shape[-1]
    wt = jnp.swapaxes(w, 1, 2)
    wide = jnp.concatenate([wt[:, :o_da], wt[:, o_gq:o_glow], wt[:, o_gg:o_mdt]], axis=1).astype(BF16)
    pad = jnp.zeros((w.shape[0], LANES - SM_USED, w.shape[1]), w.dtype)
    small = jnp.concatenate([wt[:, o_da:o_db], wt[:, o_mdt:], wt[:, o_db:o_gq], wt[:, o_glow:o_gg], pad],
                            axis=1).astype(BF16)
    return wide, small


def _small_lane_row(dn_vals, m2_vals):
    pad = jnp.zeros((LANES - SM_DB,), F32)
    return jnp.concatenate([dn_vals.astype(F32), m2_vals.astype(F32), pad]).reshape(1, LANES)


def _ssm_to_pairs(s):
    t = jnp.swapaxes(s, -1, -2)
    return t.reshape(t.shape[:-3] + (M2_PAIRS, 2 * M2_HEADDIM, M2_STATE))


def _ssm_from_pairs(s):
    t = s.reshape(s.shape[:-3] + (M2_HEADS, M2_HEADDIM, M2_STATE))
    return jnp.swapaxes(t, -1, -2)


def kernel(x_prompt, x_sample, state_conv, state_delta, state_gla, state_ssm, ffn1_norm, ffn1_w_gu, ffn1_w_down, mix_norm, w_in, conv_w, conv_b, dn_a_log, dn_dt_bias, dn_norm, gla_w_up, gla_b_up, gla_norm, m2_a_log, m2_dt_bias, m2_d, m2_norm, w_out, ffn2_norm, ffn2_w_gu, ffn2_w_down, final_norm):
    bp, sp, _ = x_prompt.shape
    bs, ss, _ = x_sample.shape
    tp, ts = bp * sp, bs * ss
    assert tp % DENSE_TILE == 0 and ts % DENSE_TILE == 0 and sp % PROMPT_CHUNK == 0
    assert bs % SAMPLE_SEQS_PER_STEP == 0 and tp % (SAMPLE_SEQS_PER_STEP * ss) == 0 and ss % SUBLANES == 0

    zero_conv = jnp.zeros((1, bp, CONV_TAIL, CONV_CH), F32)
    zero_dn = jnp.zeros((1, bp, DN_HEADS, DN_DK, DN_DV), F32)
    zero_gla = jnp.zeros((1, bp, GLA_HEADS, GLA_DK, GLA_DV), F32)
    zero_ssm = jnp.zeros((1, bp, M2_PAIRS, 2 * M2_HEADDIM, M2_STATE), F32)
    ssm_pairs = _ssm_to_pairs(state_ssm)

    wgu1, wd1, wgu2, wd2, wout = ffn1_w_gu, ffn1_w_down, ffn2_w_gu, ffn2_w_down, w_out
    win_wide, win_small = _split_w_in(w_in)
    g_ffn1, g_mix, g_ffn2 = (g.reshape(DEPTH, 1, D_MODEL) for g in (ffn1_norm, mix_norm, ffn2_norm))

    xs = [x_prompt.reshape(tp, D_MODEL), x_sample.reshape(ts, D_MODEL)]
    st_p = st_s = None
    for l in range(DEPTH):
        params = (conv_w[l], conv_b[l].reshape(1, CONV_CH),
                  _small_lane_row(dn_a_log[l], m2_a_log[l]), _small_lane_row(dn_dt_bias[l], m2_dt_bias[l]),
                  dn_norm[l].reshape(1, DN_DV),
                  gla_w_up[l].astype(BF16), gla_b_up[l].reshape(1, GLA_QK), gla_norm[l].reshape(1, GLA_DV),
                  jnp.repeat(m2_d[l], M2_HEADDIM).reshape(1, M2_DIM), m2_norm[l].reshape(1, M2_DIM))

        x = _ffn(xs, l, g_ffn1, wgu1, wd1)
        proj = _inproj(x, l, g_mix, win_wide, win_small)
        mix, *st_p = _mixer(proj, 0, bp, sp, 1, PROMPT_CHUNK, l, zero_conv, zero_dn, zero_gla, zero_ssm, params,
                            tp + ts, prev_states=st_p)
        mix, *st_s = _mixer(proj, tp, bs, ss, SAMPLE_SEQS_PER_STEP, ss, l,
                            state_conv, state_delta, state_gla, ssm_pairs, params,
                            tp + ts, prev_mix=mix, prev_states=st_s)
        if l < DEPTH - 1:
            xs = [_ffn([x], l, g_ffn2, wgu2, wd2, mix=mix, wout=wout)]
        else:
            y_prompt, y_sample = _ffn([x], l, g_ffn2, wgu2, wd2, final_gain=final_norm, split_rows=tp,
                                      mix=mix, wout=wout)

    return (y_prompt.reshape(bp, sp, D_MODEL), y_sample.reshape(bs, ss, D_MODEL),
            st_p[0], st_p[1], st_p[2], _ssm_from_pairs(st_p[3]),
            st_s[0], st_s[1], st_s[2], _ssm_from_pairs(st_s[3]))
```
